```python
import math
import jax, jax.numpy as jnp
from jax import lax
import numpy as np

D_MODEL = 1024
BATCH = 2
SEQ = 8192
DEPTH = 1

MEM_TOKENS = 256
N_MEM_HEADS = 4
DH_MEM = 128
MEM_WIDTH = N_MEM_HEADS * DH_MEM
N_DIFF_HEADS = 8
DH_DIFF = 64
DIFF_Q_WIDTH = N_DIFF_HEADS * 2 * DH_DIFF
DIFF_V_WIDTH = N_DIFF_HEADS * 2 * DH_DIFF
Q_BLOCK = 128
POOL_WINDOWS = (2, 4, 8, 16)
POOL_GROUPS = len(POOL_WINDOWS)
POOL_GROUP_WIDTH = 128
POOL_WIDTH = POOL_GROUPS * POOL_GROUP_WIDTH
N_BRANCHES = 3
IN_COLS = 2 * DIFF_Q_WIDTH + DIFF_V_WIDTH + POOL_WIDTH + MEM_WIDTH + N_BRANCHES * D_MODEL
D_FF = 2816
CONV_WIDTH = 3
NORM_EPS = 1e-6

kernel_name = "hybrid_diffattn_pool_memxattn_convffn"


def rms_norm(x, g):
    xf = x.astype(jnp.float32)
    y = xf * lax.rsqrt(jnp.mean(xf * xf, axis=-1, keepdims=True) + NORM_EPS)
    return (y * g.astype(jnp.float32)).astype(x.dtype)


def alibi_slopes(n_heads):
    return jnp.exp2(-8.0 * jnp.arange(1, n_heads + 1, dtype=jnp.float32) / n_heads)


def lambda_init_for(layer_idx):
    return 0.8 - 0.6 * math.exp(-0.3 * layer_idx)


def diff_attention(q, k, v, lam):
    B, S = q.shape[0], q.shape[1]
    n_blocks = S // Q_BLOCK
    scale = DH_DIFF ** -0.5
    slopes = alibi_slopes(N_DIFF_HEADS)
    kf = k.astype(jnp.float32)
    vf = v.astype(jnp.float32)
    kpos = jnp.arange(S)
    qb = q.astype(jnp.float32).reshape(B, n_blocks, Q_BLOCK, N_DIFF_HEADS, 2, DH_DIFF)
    qb = jnp.moveaxis(qb, 1, 0)

    def one_block(args):
        qblk, i = args
        qpos = i * Q_BLOCK + jnp.arange(Q_BLOCK)
        s = jnp.einsum('bqhcd,bkhcd->bhcqk', qblk, kf) * scale
        dist = qpos[:, None] - kpos[None, :]
        bias = -slopes[:, None, None] * dist.astype(jnp.float32)
        s = s + bias[None, :, None]
        s = jnp.where((dist >= 0)[None, None, None], s, jnp.finfo(jnp.float32).min)
        p = jax.nn.softmax(s, axis=-1)
        a = p[:, :, 0] - lam * p[:, :, 1]
        return jnp.einsum('bhqk,bkhe->bqhe', a, vf)

    out = lax.map(one_block, (qb, jnp.arange(n_blocks)))
    return jnp.moveaxis(out, 0, 1).reshape(B, S, N_DIFF_HEADS, 2 * DH_DIFF)


def multiscale_pool(u, pool_w, pool_scale):
    B, S, _ = u.shape
    uf = u.astype(jnp.float32).reshape(B, S, POOL_GROUPS, POOL_GROUP_WIDTH)
    csum = jnp.cumsum(uf, axis=1)
    cpad = jnp.concatenate([jnp.zeros_like(csum[:, :1]), csum], axis=1)
    t = jnp.arange(S)
    outs = []
    for g, w in enumerate(POOL_WINDOWS):
        lo = jnp.maximum(t + 1 - w, 0)
        window_sum = csum[:, :, g] - cpad[:, lo, g]
        count = jnp.minimum(t + 1, w).astype(jnp.float32)
        outs.append(window_sum / count[None, :, None] - uf[:, :, g])
    p = jnp.stack(outs, axis=2)
    y = jnp.einsum('bsgc,gcd->bsgd', p, pool_w.astype(jnp.float32))
    return (y.reshape(B, S, POOL_WIDTH) * pool_scale.astype(jnp.float32)).astype(u.dtype)


def memory_cross_attention(q, mem_n, w_mem_kv):
    B, S, _ = q.shape
    kv = mem_n @ w_mem_kv
    k_m, v_m = jnp.split(kv, 2, axis=-1)
    qh = q.astype(jnp.float32).reshape(B, S, N_MEM_HEADS, DH_MEM)
    kh = k_m.astype(jnp.float32).reshape(B, -1, N_MEM_HEADS, DH_MEM)
    vh = v_m.astype(jnp.float32).reshape(B, -1, N_MEM_HEADS, DH_MEM)
    s = jnp.einsum('bshd,bmhd->bhsm', qh, kh) * (DH_MEM ** -0.5)
    p = jax.nn.softmax(s, axis=-1)
    o = jnp.einsum('bhsm,bmhd->bshd', p, vh)
    return o.reshape(B, S, MEM_WIDTH).astype(q.dtype)


def causal_depthwise_conv(u, w, b):
    S = u.shape[1]
    up = jnp.pad(u, ((0, 0), (CONV_WIDTH - 1, 0), (0, 0)))
    y = b
    for j in range(CONV_WIDTH):
        y = y + w[j] * up[:, j:j + S]
    return y


def setup_inputs(seed: int = 0) -> dict:
    key = jax.random.key(seed)
    ks = jax.random.split(key, 24)

    def nrm(k, shape, scale):
        return jax.random.normal(k, shape, jnp.float32) * scale

    def gain(k, n):
        return 1.0 + nrm(k, (DEPTH, n), 0.05)

    return {
        "x": nrm(ks[0], (BATCH, SEQ, D_MODEL), 1.0),
        "mem": nrm(ks[1], (BATCH, MEM_TOKENS, D_MODEL), 1.0),
        "norm_mix_pre": gain(ks[2], D_MODEL),
        "w_in": nrm(ks[3], (DEPTH, D_MODEL, IN_COLS), D_MODEL ** -0.5),
        "lambda_q1": nrm(ks[4], (DEPTH, DH_DIFF), 0.1),
        "lambda_k1": nrm(ks[5], (DEPTH, DH_DIFF), 0.1),
        "lambda_q2": nrm(ks[6], (DEPTH, DH_DIFF), 0.1),
        "lambda_k2": nrm(ks[7], (DEPTH, DH_DIFF), 0.1),
        "subln_g": gain(ks[8], 2 * DH_DIFF),
        "w_attn_branch": nrm(ks[9], (DEPTH, DIFF_V_WIDTH, D_MODEL), DIFF_V_WIDTH ** -0.5),
        "pool_w": nrm(ks[10], (DEPTH, POOL_GROUPS, POOL_GROUP_WIDTH, POOL_GROUP_WIDTH), POOL_GROUP_WIDTH ** -0.5),
        "pool_scale": gain(ks[11], POOL_WIDTH),
        "w_pool_branch": nrm(ks[12], (DEPTH, POOL_WIDTH, D_MODEL), POOL_WIDTH ** -0.5),
        "norm_mem": gain(ks[13], D_MODEL),
        "w_mem_kv": nrm(ks[14], (DEPTH, D_MODEL, 2 * MEM_WIDTH), D_MODEL ** -0.5),
        "w_mem_branch": nrm(ks[15], (DEPTH, MEM_WIDTH, D_MODEL), MEM_WIDTH ** -0.5),
        "w_out": nrm(ks[16], (DEPTH, D_MODEL, D_MODEL), D_MODEL ** -0.5),
        "norm_mix_post": gain(ks[17], D_MODEL),
        "norm_ffn_pre": gain(ks[18], D_MODEL),
        "w_up": nrm(ks[19], (DEPTH, D_MODEL, 2 * D_FF), D_MODEL ** -0.5),
        "conv_w": nrm(ks[20], (DEPTH, CONV_WIDTH, 2 * D_FF), CONV_WIDTH ** -0.5),
        "conv_b": nrm(ks[21], (DEPTH, 2 * D_FF), 0.02),
        "w_down": nrm(ks[22], (DEPTH, D_FF, D_MODEL), D_FF ** -0.5),
        "norm_ffn_post": gain(ks[23], D_MODEL),
    }


def reference(x, mem, norm_mix_pre, w_in, lambda_q1, lambda_k1, lambda_q2, lambda_k2, subln_g,
              w_attn_branch, pool_w, pool_scale, w_pool_branch, norm_mem, w_mem_kv, w_mem_branch,
              w_out, norm_mix_post, norm_ffn_pre, w_up, conv_w, conv_b, w_down, norm_ffn_post):
    B, S, D = x.shape
    split_pts = np.cumsum([DIFF_Q_WIDTH, DIFF_Q_WIDTH, DIFF_V_WIDTH, POOL_WIDTH, MEM_WIDTH]).tolist()
    for l in range(DEPTH):
        lam_init = lambda_init_for(l)
        h = rms_norm(x, norm_mix_pre[l])
        proj = h @ w_in[l]
        q_d, k_d, v_d, u_p, q_m, gate_logits = jnp.split(proj, split_pts, axis=-1)

        lam = (jnp.exp(jnp.sum(lambda_q1[l].astype(jnp.float32) * lambda_k1[l].astype(jnp.float32)))
               - jnp.exp(jnp.sum(lambda_q2[l].astype(jnp.float32) * lambda_k2[l].astype(jnp.float32)))
               + lam_init)
        q_d = q_d.reshape(B, S, N_DIFF_HEADS, 2, DH_DIFF)
        k_d = k_d.reshape(B, S, N_DIFF_HEADS, 2, DH_DIFF)
        v_d = v_d.reshape(B, S, N_DIFF_HEADS, 2 * DH_DIFF)
        a = diff_attention(q_d, k_d, v_d, lam)
        a = rms_norm(a, subln_g[l]) * (1.0 - lam_init)
        y_attn = a.reshape(B, S, DIFF_V_WIDTH).astype(x.dtype) @ w_attn_branch[l]

        y_pool = multiscale_pool(u_p, pool_w[l], pool_scale[l]) @ w_pool_branch[l]

        mem_n = rms_norm(mem, norm_mem[l])
        y_mem = memory_cross_attention(q_m, mem_n, w_mem_kv[l]) @ w_mem_branch[l]

        gates = jax.nn.sigmoid(gate_logits.astype(jnp.float32)).reshape(B, S, N_BRANCHES, D).astype(x.dtype)
        mix = gates[:, :, 0] * y_attn + gates[:, :, 1] * y_pool + gates[:, :, 2] * y_mem
        x = x + rms_norm(mix @ w_out[l], norm_mix_post[l])

        h2 = rms_norm(x, norm_ffn_pre[l])
        up = causal_depthwise_conv(h2 @ w_up[l], conv_w[l], conv_b[l])
        g, v = jnp.split(up, 2, axis=-1)
        ff = (jax.nn.gelu(g, approximate=True) * v) @ w_down[l]
        x = x + rms_norm(ff, norm_ffn_post[l])
    return x
```

```python
import functools
import math

import jax
import jax.numpy as jnp
from jax import lax
from jax.experimental import pallas as pl
from jax.experimental.pallas import tpu as pltpu

D_MODEL = 1024
N_MEM_HEADS = 4
DH_MEM = 128
MEM_WIDTH = N_MEM_HEADS * DH_MEM
N_DIFF_HEADS = 8
DH_DIFF = 64
HEAD_W = 2 * DH_DIFF
DIFF_WIDTH = N_DIFF_HEADS * HEAD_W
POOL_WINDOWS = (2, 4, 8, 16)
POOL_GROUP_WIDTH = 128
POOL_WIDTH = len(POOL_WINDOWS) * POOL_GROUP_WIDTH
POOL_HALO = 16
N_BRANCHES = 3
IN_COLS = 3 * DIFF_WIDTH + POOL_WIDTH + MEM_WIDTH + N_BRANCHES * D_MODEL
D_FF = 2816
CONV_WIDTH = 3
CONV_HALO = 8
NORM_EPS = 1e-6
LOG2E = math.log2(math.e)
NEG_BIG = -1e30

VMEM_LIMIT_BYTES = 56 * 1024 * 1024

PROJ_TM = 1024
PROJ_TN = 1024
ATTN_BQ = 256
MERGE_TM = 256
FFN_TM = 256
FFN_CHUNK = 256


def _rms_norm(x, g):
    return x * lax.rsqrt(jnp.mean(x * x, axis=-1, keepdims=True) + NORM_EPS) * g


def _dot(a, b):
    return jnp.dot(a, b, preferred_element_type=jnp.float32)


def _dot_nt(a, b):
    return lax.dot_general(a, b, (((1,), (1,)), ((), ())), preferred_element_type=jnp.float32)


def _in_proj_kernel(x_ref, g_ref, w_ref, o_ref, h_ref, *, q_scale, n_q_tiles):
    j = pl.program_id(1)

    @pl.when(j == 0)
    def _():
        h_ref[...] = _rms_norm(x_ref[...], g_ref[...]).astype(jnp.bfloat16)

    acc = _dot(h_ref[...], w_ref[...])
    scale = jnp.where(j < n_q_tiles, jnp.float32(q_scale), jnp.float32(1.0))
    o_ref[...] = (acc * scale).astype(o_ref.dtype)


def _in_proj(x2d, g, w_bf16):
    t = x2d.shape[0]
    grid = (t // PROJ_TM, IN_COLS // PROJ_TN)
    kern = functools.partial(_in_proj_kernel, q_scale=DH_DIFF ** -0.5 * LOG2E,
                             n_q_tiles=DIFF_WIDTH // PROJ_TN)
    return pl.pallas_call(
        kern,
        grid=grid,
        in_specs=[
            pl.BlockSpec((PROJ_TM, D_MODEL), lambda i, j: (i, 0)),
            pl.BlockSpec((1, D_MODEL), lambda i, j: (0, 0)),
            pl.BlockSpec((D_MODEL, PROJ_TN), lambda i, j: (0, j)),
        ],
        out_specs=pl.BlockSpec((PROJ_TM, PROJ_TN), lambda i, j: (i, j)),
        out_shape=jax.ShapeDtypeStruct((t, IN_COLS), jnp.bfloat16),
        scratch_shapes=[pltpu.VMEM((PROJ_TM, D_MODEL), jnp.bfloat16)],
        compiler_params=pltpu.CompilerParams(
            dimension_semantics=("arbitrary", "arbitrary"), vmem_limit_bytes=VMEM_LIMIT_BYTES),
        name="in_proj",
    )(x2d, g, w_bf16)


def _mem_kv_kernel(mem_ref, g_ref, w_ref, k_ref, v_ref, *, k_scale):
    mem_n = _rms_norm(mem_ref[...], g_ref[...]).astype(jnp.bfloat16)
    kv = _dot(mem_n, w_ref[...])
    k_ref[...] = (kv[:, :MEM_WIDTH] * k_scale).astype(k_ref.dtype)
    v_ref[...] = kv[:, MEM_WIDTH:].astype(v_ref.dtype)


def _mem_kv(mem, g, w_bf16):
    b, m, _ = mem.shape
    kern = functools.partial(_mem_kv_kernel, k_scale=DH_MEM ** -0.5 * LOG2E)
    out = jax.ShapeDtypeStruct((b, m, MEM_WIDTH), jnp.bfloat16)
    return pl.pallas_call(
        kern,
        grid=(b,),
        in_specs=[
            pl.BlockSpec((None, m, D_MODEL), lambda i: (i, 0, 0)),
            pl.BlockSpec((1, D_MODEL), lambda i: (0, 0)),
            pl.BlockSpec((D_MODEL, 2 * MEM_WIDTH), lambda i: (0, 0)),
        ],
        out_specs=[pl.BlockSpec((None, m, MEM_WIDTH), lambda i: (i, 0, 0))] * 2,
        out_shape=[out, out],
        compiler_params=pltpu.CompilerParams(
            dimension_semantics=("arbitrary",), vmem_limit_bytes=VMEM_LIMIT_BYTES),
        name="mem_kv",
    )(mem, g, w_bf16)


def _diff_attn_kernel(slope_ref, lq1_ref, lk1_ref, lq2_ref, lk2_ref, subg_ref,
                      q_ref, k_ref, v_ref, o_ref,
                      qbd_ref, bias_ref, m_ref, l_ref, acc_ref, *, bq, lam_init):
    h = pl.program_id(1)
    i = pl.program_id(2)
    rows = 2 * bq
    slope = slope_ref[h]

    q = q_ref[...]
    lane = lax.broadcasted_iota(jnp.int32, q.shape, 1)
    zero = jnp.zeros_like(q)
    qbd_ref[:bq, :] = jnp.where(lane < DH_DIFF, q, zero)
    qbd_ref[bq:, :] = jnp.where(lane >= DH_DIFF, q, zero)

    row = lax.broadcasted_iota(jnp.int32, (bq, bq), 0)
    col = lax.broadcasted_iota(jnp.int32, (bq, bq), 1)
    rel = (col - row).astype(jnp.float32) * slope
    bias_ref[:bq, :] = rel
    bias_ref[bq:, :] = rel

    m_ref[...] = jnp.full(m_ref.shape, NEG_BIG, jnp.float32)
    l_ref[...] = jnp.zeros(l_ref.shape, jnp.float32)
    acc_ref[...] = jnp.zeros(acc_ref.shape, jnp.float32)

    def block(j, bias):
        start = pl.multiple_of(j * bq, bq)
        kb = k_ref[pl.ds(start, bq), :]
        vb = v_ref[pl.ds(start, bq), :]
        c = slope * ((j - i) * bq).astype(jnp.float32)
        s = _dot_nt(qbd_ref[...], kb) + bias
        m_prev = m_ref[...]
        m_next = jnp.maximum(m_prev, jnp.max(s, axis=1, keepdims=True) + c)
        p = jnp.exp2(s - (m_next[:, :1] - c))
        alpha = jnp.exp2(m_prev - m_next)
        l_ref[...] = alpha * l_ref[...] + jnp.sum(p, axis=1, keepdims=True)
        m_ref[...] = m_next
        acc_ref[...] = alpha * acc_ref[...] + _dot(p.astype(jnp.bfloat16), vb)

    def body(j, carry):
        block(j, bias_ref[...])
        return carry

    lax.fori_loop(0, i, body, 0)
    diag = jnp.where(col <= row, rel, NEG_BIG)
    block(i, jnp.concatenate([diag, diag], axis=0))

    lam = (jnp.exp(jnp.sum(lq1_ref[...] * lk1_ref[...], axis=1, keepdims=True))
           - jnp.exp(jnp.sum(lq2_ref[...] * lk2_ref[...], axis=1, keepdims=True))
           + lam_init)
    o = acc_ref[...] / l_ref[...]
    a = o[:bq] - lam * o[bq:]
    a = _rms_norm(a, subg_ref[...]) * (1.0 - lam_init)
    o_ref[...] = a.astype(o_ref.dtype)


def _diff_attn(proj, lq1, lk1, lq2, lk2, subg, lam_init):
    b, s, _ = proj.shape
    bq = ATTN_BQ
    slopes = jnp.exp2(-8.0 * jnp.arange(1, N_DIFF_HEADS + 1, dtype=jnp.float32) / N_DIFF_HEADS) * LOG2E
    kern = functools.partial(_diff_attn_kernel, bq=bq, lam_init=lam_init)
    small = lambda n: pl.BlockSpec((1, n), lambda bi, h, i: (0, 0))
    nh = N_DIFF_HEADS
    return pl.pallas_call(
        kern,
        grid=(b, nh, s // bq),
        in_specs=[
            pl.BlockSpec(memory_space=pltpu.SMEM),
            small(DH_DIFF), small(DH_DIFF), small(DH_DIFF), small(DH_DIFF), small(HEAD_W),
            pl.BlockSpec((None, bq, HEAD_W), lambda bi, h, i: (bi, i, h)),
            pl.BlockSpec((None, s, HEAD_W), lambda bi, h, i: (bi, 0, nh + h)),
            pl.BlockSpec((None, s, HEAD_W), lambda bi, h, i: (bi, 0, 2 * nh + h)),
        ],
        out_specs=pl.BlockSpec((None, bq, HEAD_W), lambda bi, h, i: (bi, i, h)),
        out_shape=jax.ShapeDtypeStruct((b, s, DIFF_WIDTH), jnp.bfloat16),
        scratch_shapes=[
            pltpu.VMEM((2 * bq, HEAD_W), jnp.bfloat16),
            pltpu.VMEM((2 * bq, bq), jnp.float32),
            pltpu.VMEM((2 * bq, HEAD_W), jnp.float32),
            pltpu.VMEM((2 * bq, HEAD_W), jnp.float32),
            pltpu.VMEM((2 * bq, HEAD_W), jnp.float32),
        ],
        compiler_params=pltpu.CompilerParams(
            dimension_semantics=("arbitrary", "arbitrary", "arbitrary"),
            vmem_limit_bytes=VMEM_LIMIT_BYTES),
        name="diff_attn",
    )(slopes, lq1, lk1, lq2, lk2, subg, proj, proj, proj)


def _merge_kernel(x_ref, a_ref, u_ref, uh_ref, qm_ref, g0_ref, g1_ref, g2_ref, km_ref, vm_ref,
                  wa_ref, pw_ref, ps_ref, wp_ref, wm_ref, wo_ref, gpost_ref, o_ref, *, tm):
    i = pl.program_id(1)

    y_attn = _dot(a_ref[...], wa_ref[...])

    u = u_ref[...].astype(jnp.float32)
    halo = jnp.where(i > 0, uh_ref[...].astype(jnp.float32), 0.0)
    t = i * tm + lax.broadcasted_iota(jnp.int32, (tm, POOL_GROUP_WIDTH), 0)
    ys = []
    for g, w in enumerate(POOL_WINDOWS):
        cols = slice(g * POOL_GROUP_WIDTH, (g + 1) * POOL_GROUP_WIDTH)
        ug = u[:, cols]
        ext = jnp.concatenate([halo[:, cols], ug], axis=0)
        wsum = ug
        for d in range(1, w):
            wsum = wsum + ext[POOL_HALO - d:POOL_HALO - d + tm]
        count = jnp.minimum(t + 1, w).astype(jnp.float32)
        pooled = wsum / count - ug
        ys.append(_dot(pooled.astype(jnp.bfloat16), pw_ref[g]))
    y = jnp.concatenate(ys, axis=1) * ps_ref[...]
    y_pool = _dot(y.astype(jnp.bfloat16), wp_ref[...])

    outs = []
    for hh in range(N_MEM_HEADS):
        cols = slice(hh * DH_MEM, (hh + 1) * DH_MEM)
        s = _dot_nt(qm_ref[:, cols], km_ref[:, cols])
        p = jnp.exp2(s - jnp.max(s, axis=1, keepdims=True))
        o = _dot(p.astype(jnp.bfloat16), vm_ref[:, cols])
        outs.append(o / jnp.sum(p, axis=1, keepdims=True))
    y_mem = _dot(jnp.concatenate(outs, axis=1).astype(jnp.bfloat16), wm_ref[...])

    mix = (jax.nn.sigmoid(g0_ref[...].astype(jnp.float32)) * y_attn
           + jax.nn.sigmoid(g1_ref[...].astype(jnp.float32)) * y_pool
           + jax.nn.sigmoid(g2_ref[...].astype(jnp.float32)) * y_mem)
    z = _dot(mix.astype(jnp.bfloat16), wo_ref[...])
    o_ref[...] = x_ref[...] + _rms_norm(z, gpost_ref[...])


def _merge(x, a, proj, km, vm, wa, pw, ps, wp, wm, wo, gpost):
    b, s, d = x.shape
    tm = MERGE_TM
    m = km.shape[1]
    halo_blocks = tm // POOL_HALO
    pool_col = 3 * DIFF_WIDTH // POOL_WIDTH
    qm_col = pool_col + 1
    gate_col = (3 * DIFF_WIDTH + POOL_WIDTH + MEM_WIDTH) // d
    const2 = lambda shape: pl.BlockSpec(shape, lambda bi, i: (0, 0))
    tile = lambda w, c: pl.BlockSpec((None, tm, w), lambda bi, i, c=c: (bi, i, c))
    kern = functools.partial(_merge_kernel, tm=tm)
    return pl.pallas_call(
        kern,
        grid=(b, s // tm),
        in_specs=[
            tile(d, 0),
            tile(DIFF_WIDTH, 0),
            tile(POOL_WIDTH, pool_col),
            pl.BlockSpec((None, POOL_HALO, POOL_WIDTH),
                         lambda bi, i: (bi, jnp.maximum(i * halo_blocks - 1, 0), pool_col)),
            tile(MEM_WIDTH, qm_col),
            tile(d, gate_col), tile(d, gate_col + 1), tile(d, gate_col + 2),
            pl.BlockSpec((None, m, MEM_WIDTH), lambda bi, i: (bi, 0, 0)),
            pl.BlockSpec((None, m, MEM_WIDTH), lambda bi, i: (bi, 0, 0)),
            const2((DIFF_WIDTH, d)),
            pl.BlockSpec(pw.shape, lambda bi, i: (0, 0, 0)),
            const2((1, POOL_WIDTH)),
            const2((POOL_WIDTH, d)),
            const2((MEM_WIDTH, d)),
            const2((d, d)),
            const2((1, d)),
        ],
        out_specs=tile(d, 0),
        out_shape=jax.ShapeDtypeStruct((b, s, d), jnp.float32),
        compiler_params=pltpu.CompilerParams(
            dimension_semantics=("arbitrary", "arbitrary"), vmem_limit_bytes=VMEM_LIMIT_BYTES),
        name="merge",
    )(x, a, proj, proj, proj, proj, proj, proj, km, vm, wa, pw, ps, wp, wm, wo, gpost)


def _conv_ffn_kernel(x_ref, gpre_ref, wup_ref, cw_ref, cb_ref, wdn_ref, gpost_ref, o_ref,
                     carry_ref, act_ref, *, tm, chunk):
    i = pl.program_id(1)

    @pl.when(i == 0)
    def _():
        carry_ref[...] = jnp.zeros(carry_ref.shape, jnp.float32)

    x = x_ref[...]
    h = _rms_norm(x, gpre_ref[...]).astype(jnp.bfloat16)

    def conv(c0):
        up = _dot(h, wup_ref[:, c0:c0 + chunk])
        ext = jnp.concatenate([carry_ref[:, c0:c0 + chunk], up], axis=0)
        carry_ref[:, c0:c0 + chunk] = up[tm - CONV_HALO:]
        w = cw_ref[:, c0:c0 + chunk]
        y = cb_ref[:, c0:c0 + chunk] + w[2:3] * up
        for j in range(CONV_WIDTH - 1):
            off = CONV_HALO - (CONV_WIDTH - 1) + j
            y = y + w[j:j + 1] * ext[off:off + tm]
        return y

    for c0 in range(0, D_FF, chunk):
        gate = conv(c0)
        val = conv(D_FF + c0)
        act_ref[:, c0:c0 + chunk] = (jax.nn.gelu(gate, approximate=True) * val).astype(jnp.bfloat16)

    ff = _dot(act_ref[...], wdn_ref[...])
    o_ref[...] = x + _rms_norm(ff, gpost_ref[...])


def _conv_ffn(x, gpre, wup, cw, cb, wdn, gpost):
    b, s, d = x.shape
    tm = FFN_TM
    const2 = lambda shape: pl.BlockSpec(shape, lambda bi, i: (0, 0))
    tile = pl.BlockSpec((None, tm, d), lambda bi, i: (bi, i, 0))
    kern = functools.partial(_conv_ffn_kernel, tm=tm, chunk=FFN_CHUNK)
    return pl.pallas_call(
        kern,
        grid=(b, s // tm),
        in_specs=[
            tile,
            const2((1, d)),
            const2((d, 2 * D_FF)),
            const2((CONV_WIDTH, 2 * D_FF)),
            const2((1, 2 * D_FF)),
            const2((D_FF, d)),
            const2((1, d)),
        ],
        out_specs=tile,
        out_shape=jax.ShapeDtypeStruct((b, s, d), jnp.float32),
        scratch_shapes=[
            pltpu.VMEM((CONV_HALO, 2 * D_FF), jnp.float32),
            pltpu.VMEM((tm, D_FF), jnp.bfloat16),
        ],
        compiler_params=pltpu.CompilerParams(
            dimension_semantics=("arbitrary", "arbitrary"), vmem_limit_bytes=VMEM_LIMIT_BYTES),
        name="conv_ffn",
    )(x, gpre, wup, cw, cb, wdn, gpost)


def kernel(x, mem, norm_mix_pre, w_in, lambda_q1, lambda_k1, lambda_q2, lambda_k2, subln_g, w_attn_branch, pool_w, pool_scale, w_pool_branch, norm_mem, w_mem_kv, w_mem_branch, w_out, norm_mix_post, norm_ffn_pre, w_up, conv_w, conv_b, w_down, norm_ffn_post):
    b, s, d = x.shape
    depth = w_in.shape[0]
    bf = lambda w: w.astype(jnp.bfloat16)
    for l in range(depth):
        lam_init = 0.8 - 0.6 * math.exp(-0.3 * l)
        proj = _in_proj(x.reshape(b * s, d), norm_mix_pre[l][None], bf(w_in[l])).reshape(b, s, IN_COLS)
        km, vm = _mem_kv(mem, norm_mem[l][None], bf(w_mem_kv[l]))
        a = _diff_attn(proj, lambda_q1[l][None], lambda_k1[l][None], lambda_q2[l][None],
                       lambda_k2[l][None], subln_g[l][None], lam_init)
        x = _merge(x, a, proj, km, vm, bf(w_attn_branch[l]), bf(pool_w[l]), pool_scale[l][None],
                   bf(w_pool_branch[l]), bf(w_mem_branch[l]), bf(w_out[l]), norm_mix_post[l][None])
        x = _conv_ffn(x, norm_ffn_pre[l][None], bf(w_up[l]), conv_w[l], conv_b[l][None],
                      bf(w_down[l]), norm_ffn_post[l][None])
    return x
```

```python
import functools
import math

import jax
import jax.numpy as jnp
from jax import lax
from jax.experimental import pallas as pl
from jax.experimental.pallas import tpu as pltpu

D_MODEL = 1024
N_MEM_HEADS = 4
DH_MEM = 128
MEM_WIDTH = N_MEM_HEADS * DH_MEM
N_DIFF_HEADS = 8
DH_DIFF = 64
HEAD_W = 2 * DH_DIFF
DIFF_WIDTH = N_DIFF_HEADS * HEAD_W
POOL_WINDOWS = (2, 4, 8, 16)
POOL_GROUP_WIDTH = 128
POOL_WIDTH = len(POOL_WINDOWS) * POOL_GROUP_WIDTH
POOL_HALO = 16
N_BRANCHES = 3
IN_COLS = 3 * DIFF_WIDTH + POOL_WIDTH + MEM_WIDTH + N_BRANCHES * D_MODEL
D_FF = 2816
CONV_WIDTH = 3
CONV_HALO = 8
NORM_EPS = 1e-6
LOG2E = math.log2(math.e)
NEG_BIG = -1e30

VMEM_LIMIT_BYTES = 56 * 1024 * 1024

PROJ_TM = 1024
PROJ_TN = 1024
ATTN_BQ = 1024
ATTN_BK = 256
ATTN_BLOCKS_PER_ITER = 2
ATTN_SCORES_AHEAD = 4
ATTN_CW = 256
MERGE_TM = 256
FFN_TM = 256
FFN_CHUNK = 256


def _rms_norm(x, g):
    return x * lax.rsqrt(jnp.mean(x * x, axis=-1, keepdims=True) + NORM_EPS) * g


def _dot(a, b):
    return jnp.dot(a, b, preferred_element_type=jnp.float32)


def _dot_nt(a, b):
    return lax.dot_general(a, b, (((1,), (1,)), ((), ())), preferred_element_type=jnp.float32)


def _in_proj_kernel(x_ref, g_ref, w_ref, o_ref, h_ref, *, q_scale, n_q_tiles):
    j = pl.program_id(1)

    @pl.when(j == 0)
    def _():
        h_ref[...] = _rms_norm(x_ref[...], g_ref[...]).astype(jnp.bfloat16)

    acc = _dot(h_ref[...], w_ref[...])
    scale = jnp.where(j < n_q_tiles, jnp.float32(q_scale), jnp.float32(1.0))
    o_ref[...] = (acc * scale).astype(o_ref.dtype)


def _in_proj(x2d, g, w_bf16):
    t = x2d.shape[0]
    grid = (t // PROJ_TM, IN_COLS // PROJ_TN)
    kern = functools.partial(_in_proj_kernel, q_scale=DH_DIFF ** -0.5 * LOG2E,
                             n_q_tiles=DIFF_WIDTH // PROJ_TN)
    return pl.pallas_call(
        kern,
        grid=grid,
        in_specs=[
            pl.BlockSpec((PROJ_TM, D_MODEL), lambda i, j: (i, 0)),
            pl.BlockSpec((1, D_MODEL), lambda i, j: (0, 0)),
            pl.BlockSpec((D_MODEL, PROJ_TN), lambda i, j: (0, j)),
        ],
        out_specs=pl.BlockSpec((PROJ_TM, PROJ_TN), lambda i, j: (i, j)),
        out_shape=jax.ShapeDtypeStruct((t, IN_COLS), jnp.bfloat16),
        scratch_shapes=[pltpu.VMEM((PROJ_TM, D_MODEL), jnp.bfloat16)],
        compiler_params=pltpu.CompilerParams(
            dimension_semantics=("arbitrary", "arbitrary"), vmem_limit_bytes=VMEM_LIMIT_BYTES),
        name="in_proj",
    )(x2d, g, w_bf16)


def _mem_kv_kernel(mem_ref, g_ref, w_ref, k_ref, v_ref, *, k_scale):
    mem_n = _rms_norm(mem_ref[...], g_ref[...]).astype(jnp.bfloat16)
    kv = _dot(mem_n, w_ref[...])
    k_ref[...] = (kv[:, :MEM_WIDTH] * k_scale).astype(k_ref.dtype)
    v_ref[...] = kv[:, MEM_WIDTH:].astype(v_ref.dtype)


def _mem_kv(mem, g, w_bf16):
    b, m, _ = mem.shape
    kern = functools.partial(_mem_kv_kernel, k_scale=DH_MEM ** -0.5 * LOG2E)
    out = jax.ShapeDtypeStruct((b, m, MEM_WIDTH), jnp.bfloat16)
    return pl.pallas_call(
        kern,
        grid=(b,),
        in_specs=[
            pl.BlockSpec((None, m, D_MODEL), lambda i: (i, 0, 0)),
            pl.BlockSpec((1, D_MODEL), lambda i: (0, 0)),
            pl.BlockSpec((D_MODEL, 2 * MEM_WIDTH), lambda i: (0, 0)),
        ],
        out_specs=[pl.BlockSpec((None, m, MEM_WIDTH), lambda i: (i, 0, 0))] * 2,
        out_shape=[out, out],
        compiler_params=pltpu.CompilerParams(
            dimension_semantics=("arbitrary",), vmem_limit_bytes=VMEM_LIMIT_BYTES),
        name="mem_kv",
    )(mem, g, w_bf16)


_SKIP = "skip"


def _diff_attn_kernel(slope_ref, lq1_ref, lk1_ref, lq2_ref, lk2_ref, subg_ref,
                      qt_ref, k_ref, vt_ref, o_ref,
                      qbd_ref, bias_ref, m_ref, l_ref, acc_ref, *, bq, bk, lam_init):
    h = pl.program_id(1)
    i = pl.program_id(2)
    cw = ATTN_CW
    n_sub = bq // cw
    slope = slope_ref[h]

    qt = qt_ref[...]
    feat = lax.broadcasted_iota(jnp.int32, qt.shape, 0)
    zero = jnp.zeros_like(qt)
    qbd_ref[:, :bq] = jnp.where(feat < DH_DIFF, qt, zero)
    qbd_ref[:, bq:] = jnp.where(feat >= DH_DIFF, qt, zero)

    key = lax.broadcasted_iota(jnp.int32, (bk, bq), 0)
    qry = lax.broadcasted_iota(jnp.int32, (bk, bq), 1)
    bias_ref[...] = (key - qry).astype(jnp.float32) * slope

    m_ref[...] = jnp.full(m_ref.shape, NEG_BIG, jnp.float32)
    l_ref[...] = jnp.zeros(l_ref.shape, jnp.float32)
    acc_ref[...] = jnp.zeros(acc_ref.shape, jnp.float32)

    def run_blocks(blocks):
        steps = []
        for blk, c, masks in blocks:
            kb = k_ref[pl.ds(pl.multiple_of(blk * bk, bk), bk), :]
            vtb = vt_ref[blk]
            steps += [(kb, vtb, c, ch, masks[ch % n_sub])
                      for ch in range(2 * n_sub) if masks[ch % n_sub] is not _SKIP]

        def scores(step):
            kb, _, _, ch, mask = step
            qs = ch % n_sub
            s = _dot(kb, qbd_ref[:, ch * cw:(ch + 1) * cw]) + bias_ref[:, qs * cw:(qs + 1) * cw]
            return s if mask is None else jnp.where(mask, s, NEG_BIG)

        def update(step, s):
            _, vtb, c, ch, _ = step
            lanes = slice(ch * cw, (ch + 1) * cw)
            m_prev = m_ref[:, lanes]
            m_next = jnp.maximum(m_prev, jnp.max(s, axis=0, keepdims=True) + c)
            p = jnp.exp2(s - (m_next - c))
            alpha = jnp.exp2(m_prev - m_next)
            l_ref[:, lanes] = alpha * l_ref[:, lanes] + jnp.sum(p, axis=0, keepdims=True)
            m_ref[:, lanes] = m_next
            acc_ref[:, lanes] = alpha * acc_ref[:, lanes] + _dot(vtb, p.astype(jnp.bfloat16))

        ahead = ATTN_SCORES_AHEAD
        pending = [scores(step) for step in steps[:ahead]]
        for n, step in enumerate(steps):
            if n + ahead < len(steps):
                pending.append(scores(steps[n + ahead]))
            update(step, pending.pop(0))

    blocks_per_tile = bq // bk
    unroll = ATTN_BLOCKS_PER_ITER

    def body(jj, carry):
        blocks = []
        for u in range(unroll):
            j = jj * unroll + u
            blocks.append((j, slope * (j * bk - i * bq).astype(jnp.float32), [None] * n_sub))
        run_blocks(blocks)
        return carry

    lax.fori_loop(0, i * (blocks_per_tile // unroll), body, 0)

    kk = lax.broadcasted_iota(jnp.int32, (bk, cw), 0)
    qq = lax.broadcasted_iota(jnp.int32, (bk, cw), 1)
    diag_blocks = []
    for d in range(blocks_per_tile):
        masks = []
        for qs in range(n_sub):
            lo_key, hi_key = d * bk, d * bk + bk - 1
            lo_q, hi_q = qs * cw, qs * cw + cw - 1
            if hi_key <= lo_q:
                masks.append(None)
            elif lo_key > hi_q:
                masks.append(_SKIP)
            else:
                masks.append(kk + lo_key <= qq + lo_q)
        diag_blocks.append((i * blocks_per_tile + d, slope * float(d * bk), masks))
    run_blocks(diag_blocks)

    lam = (jnp.exp(jnp.sum(lq1_ref[...] * lk1_ref[...], axis=1, keepdims=True))
           - jnp.exp(jnp.sum(lq2_ref[...] * lk2_ref[...], axis=1, keepdims=True))
           + lam_init)
    o = acc_ref[...] / l_ref[...]
    a = o[:, :bq] - lam * o[:, bq:]
    a = a * lax.rsqrt(jnp.mean(a * a, axis=0, keepdims=True) + NORM_EPS) * subg_ref[...] * (1.0 - lam_init)
    o_ref[...] = a.T.astype(o_ref.dtype)


def _diff_attn(proj, lq1, lk1, lq2, lk2, subg, lam_init):
    b, s, _ = proj.shape
    bq, bk, nh = ATTN_BQ, ATTN_BK, N_DIFF_HEADS
    slopes = jnp.exp2(-8.0 * jnp.arange(1, nh + 1, dtype=jnp.float32) / nh) * LOG2E
    qt = proj[:, :, :DIFF_WIDTH].reshape(b, s, nh, HEAD_W).transpose(0, 2, 3, 1)
    vt = proj[:, :, 2 * DIFF_WIDTH:3 * DIFF_WIDTH].reshape(b, s // bk, bk, nh, HEAD_W)
    vt = vt.transpose(0, 3, 1, 4, 2)
    kern = functools.partial(_diff_attn_kernel, bq=bq, bk=bk, lam_init=lam_init)
    small = lambda n: pl.BlockSpec((1, n), lambda bi, h, i: (0, 0))
    return pl.pallas_call(
        kern,
        grid=(b, nh, s // bq),
        in_specs=[
            pl.BlockSpec(memory_space=pltpu.SMEM),
            small(DH_DIFF), small(DH_DIFF), small(DH_DIFF), small(DH_DIFF),
            pl.BlockSpec((HEAD_W, 1), lambda bi, h, i: (0, 0)),
            pl.BlockSpec((None, None, HEAD_W, bq), lambda bi, h, i: (bi, h, 0, i)),
            pl.BlockSpec((None, s, HEAD_W), lambda bi, h, i: (bi, 0, nh + h)),
            pl.BlockSpec((None, None, s // bk, HEAD_W, bk), lambda bi, h, i: (bi, h, 0, 0, 0)),
        ],
        out_specs=pl.BlockSpec((None, bq, HEAD_W), lambda bi, h, i: (bi, i, h)),
        out_shape=jax.ShapeDtypeStruct((b, s, DIFF_WIDTH), jnp.bfloat16),
        scratch_shapes=[
            pltpu.VMEM((HEAD_W, 2 * bq), jnp.bfloat16),
            pltpu.VMEM((bk, bq), jnp.float32),
            pltpu.VMEM((1, 2 * bq), jnp.float32),
            pltpu.VMEM((1, 2 * bq), jnp.float32),
            pltpu.VMEM((HEAD_W, 2 * bq), jnp.float32),
        ],
        compiler_params=pltpu.CompilerParams(
            dimension_semantics=("arbitrary", "arbitrary", "arbitrary"),
            vmem_limit_bytes=VMEM_LIMIT_BYTES),
        name="diff_attn",
    )(slopes, lq1, lk1, lq2, lk2, subg, qt, proj, vt)


def _merge_kernel(x_ref, a_ref, u_ref, uh_ref, qm_ref, g0_ref, g1_ref, g2_ref, km_ref, vm_ref,
                  wa_ref, pw_ref, ps_ref, wp_ref, wm_ref, wo_ref, gpost_ref, o_ref, *, tm):
    i = pl.program_id(1)

    y_attn = _dot(a_ref[...], wa_ref[...])

    u = u_ref[...].astype(jnp.float32)
    halo = jnp.where(i > 0, uh_ref[...].astype(jnp.float32), 0.0)
    t = i * tm + lax.broadcasted_iota(jnp.int32, (tm, POOL_GROUP_WIDTH), 0)
    ys = []
    for g, w in enumerate(POOL_WINDOWS):
        cols = slice(g * POOL_GROUP_WIDTH, (g + 1) * POOL_GROUP_WIDTH)
        ug = u[:, cols]
        ext = jnp.concatenate([halo[:, cols], ug], axis=0)
        wsum = ug
        for d in range(1, w):
            wsum = wsum + ext[POOL_HALO - d:POOL_HALO - d + tm]
        count = jnp.minimum(t + 1, w).astype(jnp.float32)
        pooled = wsum / count - ug
        ys.append(_dot(pooled.astype(jnp.bfloat16), pw_ref[g]))
    y = jnp.concatenate(ys, axis=1) * ps_ref[...]
    y_pool = _dot(y.astype(jnp.bfloat16), wp_ref[...])

    outs = []
    for hh in range(N_MEM_HEADS):
        cols = slice(hh * DH_MEM, (hh + 1) * DH_MEM)
        s = _dot_nt(qm_ref[:, cols], km_ref[:, cols])
        p = jnp.exp2(s - jnp.max(s, axis=1, keepdims=True))
        o = _dot(p.astype(jnp.bfloat16), vm_ref[:, cols])
        outs.append(o / jnp.sum(p, axis=1, keepdims=True))
    y_mem = _dot(jnp.concatenate(outs, axis=1).astype(jnp.bfloat16), wm_ref[...])

    mix = (jax.nn.sigmoid(g0_ref[...].astype(jnp.float32)) * y_attn
           + jax.nn.sigmoid(g1_ref[...].astype(jnp.float32)) * y_pool
           + jax.nn.sigmoid(g2_ref[...].astype(jnp.float32)) * y_mem)
    z = _dot(mix.astype(jnp.bfloat16), wo_ref[...])
    o_ref[...] = x_ref[...] + _rms_norm(z, gpost_ref[...])


def _merge(x, a, proj, km, vm, wa, pw, ps, wp, wm, wo, gpost):
    b, s, d = x.shape
    tm = MERGE_TM
    m = km.shape[1]
    halo_blocks = tm // POOL_HALO
    pool_col = 3 * DIFF_WIDTH // POOL_WIDTH
    qm_col = pool_col + 1
    gate_col = (3 * DIFF_WIDTH + POOL_WIDTH + MEM_WIDTH) // d
    const2 = lambda shape: pl.BlockSpec(shape, lambda bi, i: (0, 0))
    tile = lambda w, c: pl.BlockSpec((None, tm, w), lambda bi, i, c=c: (bi, i, c))
    kern = functools.partial(_merge_kernel, tm=tm)
    return pl.pallas_call(
        kern,
        grid=(b, s // tm),
        in_specs=[
            tile(d, 0),
            tile(DIFF_WIDTH, 0),
            tile(POOL_WIDTH, pool_col),
            pl.BlockSpec((None, POOL_HALO, POOL_WIDTH),
                         lambda bi, i: (bi, jnp.maximum(i * halo_blocks - 1, 0), pool_col)),
            tile(MEM_WIDTH, qm_col),
            tile(d, gate_col), tile(d, gate_col + 1), tile(d, gate_col + 2),
            pl.BlockSpec((None, m, MEM_WIDTH), lambda bi, i: (bi, 0, 0)),
            pl.BlockSpec((None, m, MEM_WIDTH), lambda bi, i: (bi, 0, 0)),
            const2((DIFF_WIDTH, d)),
            pl.BlockSpec(pw.shape, lambda bi, i: (0, 0, 0)),
            const2((1, POOL_WIDTH)),
            const2((POOL_WIDTH, d)),
            const2((MEM_WIDTH, d)),
            const2((d, d)),
            const2((1, d)),
        ],
        out_specs=tile(d, 0),
        out_shape=jax.ShapeDtypeStruct((b, s, d), jnp.float32),
        compiler_params=pltpu.CompilerParams(
            dimension_semantics=("arbitrary", "arbitrary"), vmem_limit_bytes=VMEM_LIMIT_BYTES),
        name="merge",
    )(x, a, proj, proj, proj, proj, proj, proj, km, vm, wa, pw, ps, wp, wm, wo, gpost)


def _conv_ffn_kernel(x_ref, gpre_ref, wup_ref, cw_ref, cb_ref, wdn_ref, gpost_ref, o_ref,
                     carry_ref, act_ref, *, tm, chunk):
    i = pl.program_id(1)

    @pl.when(i == 0)
    def _():
        carry_ref[...] = jnp.zeros(carry_ref.shape, jnp.float32)

    x = x_ref[...]
    h = _rms_norm(x, gpre_ref[...]).astype(jnp.bfloat16)

    def conv(c0):
        up = _dot(h, wup_ref[:, c0:c0 + chunk])
        ext = jnp.concatenate([carry_ref[:, c0:c0 + chunk], up], axis=0)
        carry_ref[:, c0:c0 + chunk] = up[tm - CONV_HALO:]
        w = cw_ref[:, c0:c0 + chunk]
        y = cb_ref[:, c0:c0 + chunk] + w[2:3] * up
        for j in range(CONV_WIDTH - 1):
            off = CONV_HALO - (CONV_WIDTH - 1) + j
            y = y + w[j:j + 1] * ext[off:off + tm]
        return y

    for c0 in range(0, D_FF, chunk):
        gate = conv(c0)
        val = conv(D_FF + c0)
        act_ref[:, c0:c0 + chunk] = (jax.nn.gelu(gate, approximate=True) * val).astype(jnp.bfloat16)

    ff = _dot(act_ref[...], wdn_ref[...])
    o_ref[...] = x + _rms_norm(ff, gpost_ref[...])


def _conv_ffn(x, gpre, wup, cw, cb, wdn, gpost):
    b, s, d = x.shape
    tm = FFN_TM
    const2 = lambda shape: pl.BlockSpec(shape, lambda bi, i: (0, 0))
    tile = pl.BlockSpec((None, tm, d), lambda bi, i: (bi, i, 0))
    kern = functools.partial(_conv_ffn_kernel, tm=tm, chunk=FFN_CHUNK)
    return pl.pallas_call(
        kern,
        grid=(b, s // tm),
        in_specs=[
            tile,
            const2((1, d)),
            const2((d, 2 * D_FF)),
            const2((CONV_WIDTH, 2 * D_FF)),
            const2((1, 2 * D_FF)),
            const2((D_FF, d)),
            const2((1, d)),
        ],
        out_specs=tile,
        out_shape=jax.ShapeDtypeStruct((b, s, d), jnp.float32),
        scratch_shapes=[
            pltpu.VMEM((CONV_HALO, 2 * D_FF), jnp.float32),
            pltpu.VMEM((tm, D_FF), jnp.bfloat16),
        ],
        compiler_params=pltpu.CompilerParams(
            dimension_semantics=("arbitrary", "arbitrary"), vmem_limit_bytes=VMEM_LIMIT_BYTES),
        name="conv_ffn",
    )(x, gpre, wup, cw, cb, wdn, gpost)


def kernel(x, mem, norm_mix_pre, w_in, lambda_q1, lambda_k1, lambda_q2, lambda_k2, subln_g, w_attn_branch, pool_w, pool_scale, w_pool_branch, norm_mem, w_mem_kv, w_mem_branch, w_out, norm_mix_post, norm_ffn_pre, w_up, conv_w, conv_b, w_down, norm_ffn_post):
    b, s, d = x.shape
    depth = w_in.shape[0]
    bf = lambda w: w.astype(jnp.bfloat16)
    for l in range(depth):
        lam_init = 0.8 - 0.6 * math.exp(-0.3 * l)
        proj = _in_proj(x.reshape(b * s, d), norm_mix_pre[l][None], bf(w_in[l])).reshape(b, s, IN_COLS)
        km, vm = _mem_kv(mem, norm_mem[l][None], bf(w_mem_kv[l]))
        a = _diff_attn(proj, lambda_q1[l][None], lambda_k1[l][None], lambda_q2[l][None],
                       lambda_k2[l][None], subln_g[l][:, None], lam_init)
        x = _merge(x, a, proj, km, vm, bf(w_attn_branch[l]), bf(pool_w[l]), pool_scale[l][None],
                   bf(w_pool_branch[l]), bf(w_mem_branch[l]), bf(w_out[l]), norm_mix_post[l][None])
        x = _conv_ffn(x, norm_ffn_pre[l][None], bf(w_up[l]), conv_w[l], conv_b[l][None],
                      bf(w_down[l]), norm_ffn_post[l][None])
    return x
```

```python
import functools
import math

import jax
import jax.numpy as jnp
from jax import lax
from jax.experimental import pallas as pl
from jax.experimental.pallas import tpu as pltpu

D_MODEL = 1024
N_MEM_HEADS = 4
DH_MEM = 128
MEM_WIDTH = N_MEM_HEADS * DH_MEM
N_DIFF_HEADS = 8
DH_DIFF = 64
HEAD_W = 2 * DH_DIFF
DIFF_WIDTH = N_DIFF_HEADS * HEAD_W
POOL_WINDOWS = (2, 4, 8, 16)
POOL_GROUP_WIDTH = 128
POOL_WIDTH = len(POOL_WINDOWS) * POOL_GROUP_WIDTH
POOL_HALO = 16
N_BRANCHES = 3
IN_COLS = 3 * DIFF_WIDTH + POOL_WIDTH + MEM_WIDTH + N_BRANCHES * D_MODEL
D_FF = 2816
CONV_WIDTH = 3
CONV_HALO = 8
NORM_EPS = 1e-6
LOG2E = math.log2(math.e)
NEG_BIG = -1e30

VMEM_LIMIT_BYTES = 56 * 1024 * 1024

PROJ_TM = 1024
PROJ_TN = 1024
ATTN_BQ = 2048
ATTN_BK = 256
ATTN_BLOCKS_PER_ITER = 4
ATTN_SCORES_AHEAD = 4
ATTN_CW = 256
BIAS_ROWS = 16
ACC_PAD = 16
MERGE_TM = 256
FFN_TM = 256
FFN_CHUNK = 256


def _rms_norm(x, g):
    return x * lax.rsqrt(jnp.mean(x * x, axis=-1, keepdims=True) + NORM_EPS) * g


def _dot(a, b):
    return jnp.dot(a, b, preferred_element_type=jnp.float32)


def _dot_nt(a, b):
    return lax.dot_general(a, b, (((1,), (1,)), ((), ())), preferred_element_type=jnp.float32)


def _in_proj_kernel(x_ref, g_ref, w_ref, o_ref, h_ref, *, q_scale, n_q_tiles):
    j = pl.program_id(1)

    @pl.when(j == 0)
    def _():
        h_ref[...] = _rms_norm(x_ref[...], g_ref[...]).astype(jnp.bfloat16)

    acc = _dot(h_ref[...], w_ref[...])
    scale = jnp.where(j < n_q_tiles, jnp.float32(q_scale), jnp.float32(1.0))
    o_ref[...] = (acc * scale).astype(o_ref.dtype)


def _in_proj(x2d, g, w_bf16):
    t = x2d.shape[0]
    grid = (t // PROJ_TM, IN_COLS // PROJ_TN)
    kern = functools.partial(_in_proj_kernel, q_scale=DH_DIFF ** -0.5 * LOG2E,
                             n_q_tiles=DIFF_WIDTH // PROJ_TN)
    return pl.pallas_call(
        kern,
        grid=grid,
        in_specs=[
            pl.BlockSpec((PROJ_TM, D_MODEL), lambda i, j: (i, 0)),
            pl.BlockSpec((1, D_MODEL), lambda i, j: (0, 0)),
            pl.BlockSpec((D_MODEL, PROJ_TN), lambda i, j: (0, j)),
        ],
        out_specs=pl.BlockSpec((PROJ_TM, PROJ_TN), lambda i, j: (i, j)),
        out_shape=jax.ShapeDtypeStruct((t, IN_COLS), jnp.bfloat16),
        scratch_shapes=[pltpu.VMEM((PROJ_TM, D_MODEL), jnp.bfloat16)],
        compiler_params=pltpu.CompilerParams(
            dimension_semantics=("arbitrary", "arbitrary"), vmem_limit_bytes=VMEM_LIMIT_BYTES),
        name="in_proj",
    )(x2d, g, w_bf16)


def _mem_kv_kernel(mem_ref, g_ref, w_ref, k_ref, v_ref, *, k_scale):
    mem_n = _rms_norm(mem_ref[...], g_ref[...]).astype(jnp.bfloat16)
    kv = _dot(mem_n, w_ref[...])
    k_ref[...] = (kv[:, :MEM_WIDTH] * k_scale).astype(k_ref.dtype)
    v_ref[...] = kv[:, MEM_WIDTH:].astype(v_ref.dtype)


def _mem_kv(mem, g, w_bf16):
    b, m, _ = mem.shape
    kern = functools.partial(_mem_kv_kernel, k_scale=DH_MEM ** -0.5 * LOG2E)
    out = jax.ShapeDtypeStruct((b, m, MEM_WIDTH), jnp.bfloat16)
    return pl.pallas_call(
        kern,
        grid=(b,),
        in_specs=[
            pl.BlockSpec((None, m, D_MODEL), lambda i: (i, 0, 0)),
            pl.BlockSpec((1, D_MODEL), lambda i: (0, 0)),
            pl.BlockSpec((D_MODEL, 2 * MEM_WIDTH), lambda i: (0, 0)),
        ],
        out_specs=[pl.BlockSpec((None, m, MEM_WIDTH), lambda i: (i, 0, 0))] * 2,
        out_shape=[out, out],
        compiler_params=pltpu.CompilerParams(
            dimension_semantics=("arbitrary",), vmem_limit_bytes=VMEM_LIMIT_BYTES),
        name="mem_kv",
    )(mem, g, w_bf16)


_SKIP = "skip"


def _diff_attn_kernel(slope_ref, lq1_ref, lk1_ref, lq2_ref, lk2_ref, subg_ref,
                      qt_ref, k_ref, vt_ref, o_ref,
                      qbd_ref, kfeat_ref, m_ref, acc_ref, *, bq, bk, lam_init):
    h = pl.program_id(1)
    i = pl.program_id(2)
    cw = ATTN_CW
    n_sub = bq // cw
    slope = slope_ref[h]

    def split3(x):
        a = x.astype(jnp.bfloat16).astype(jnp.float32)
        b = (x - a).astype(jnp.bfloat16).astype(jnp.float32)
        return a, b, x - a - b

    qt = qt_ref[...]
    feat = lax.broadcasted_iota(jnp.int32, qt.shape, 0)
    zero = jnp.zeros_like(qt)
    qbd_ref[:HEAD_W, :bq] = jnp.where(feat < DH_DIFF, qt, zero)
    qbd_ref[:HEAD_W, bq:] = jnp.where(feat >= DH_DIFF, qt, zero)
    r = lax.broadcasted_iota(jnp.int32, (BIAS_ROWS, bq), 0)
    ii = lax.broadcasted_iota(jnp.int32, (BIAS_ROWS, bq), 1).astype(jnp.float32)
    pieces = split3(jnp.full((BIAS_ROWS, bq), slope, jnp.float32)) + split3(-ii * slope)
    rows = jnp.zeros((BIAS_ROWS, bq), jnp.float32)
    for n, piece in enumerate(pieces):
        rows = jnp.where(r == n, piece, rows)
    rows = rows.astype(jnp.bfloat16)
    qbd_ref[HEAD_W:HEAD_W + BIAS_ROWS, :bq] = rows
    qbd_ref[HEAD_W:HEAD_W + BIAS_ROWS, bq:] = rows
    qbd_ref[HEAD_W + BIAS_ROWS:, :] = jnp.zeros((HEAD_W - BIAS_ROWS, 2 * bq), jnp.bfloat16)

    jj = lax.broadcasted_iota(jnp.int32, (bk, HEAD_W), 0).astype(jnp.float32)
    fcol = lax.broadcasted_iota(jnp.int32, (bk, HEAD_W), 1)
    kfeat_ref[...] = jnp.where(fcol < 3, jj, jnp.where(fcol < 6, 1.0, 0.0)).astype(jnp.bfloat16)

    m_ref[...] = jnp.full(m_ref.shape, NEG_BIG, jnp.float32)
    acc_ref[...] = jnp.zeros(acc_ref.shape, jnp.float32)
    orow = lax.broadcasted_iota(jnp.int32, (ACC_PAD, bk), 0)
    ones_rows = jnp.where(orow == 0, 1.0, 0.0).astype(jnp.bfloat16)

    def run_blocks(blocks):
        steps = []
        for blk, c, masks in blocks:
            kb = k_ref[pl.ds(pl.multiple_of(blk * bk, bk), bk), :]
            kb = jnp.concatenate([kb, kfeat_ref[...]], axis=1)
            vtb = jnp.concatenate([vt_ref[blk], ones_rows], axis=0)
            steps += [(kb, vtb, c, ch, masks[ch % n_sub])
                      for ch in range(2 * n_sub) if masks[ch % n_sub] is not _SKIP]

        def scores(step):
            kb, _, _, ch, mask = step
            s = _dot(kb, qbd_ref[:, ch * cw:(ch + 1) * cw])
            return s if mask is None else jnp.where(mask, s, NEG_BIG)

        def update(step, s):
            _, vtb, c, ch, _ = step
            lanes = slice(ch * cw, (ch + 1) * cw)
            m_prev = m_ref[:, lanes]
            m_next = jnp.maximum(m_prev, jnp.max(s, axis=0, keepdims=True) + c)
            p = jnp.exp2(s - (m_next - c))
            alpha = jnp.exp2(m_prev - m_next)
            m_ref[:, lanes] = m_next
            acc_ref[:, lanes] = alpha * acc_ref[:, lanes] + _dot(vtb, p.astype(jnp.bfloat16))

        ahead = ATTN_SCORES_AHEAD
        pending = [scores(step) for step in steps[:ahead]]
        for n, step in enumerate(steps):
            if n + ahead < len(steps):
                pending.append(scores(steps[n + ahead]))
            update(step, pending.pop(0))

    blocks_per_tile = bq // bk
    unroll = ATTN_BLOCKS_PER_ITER

    def body(jj, carry):
        blocks = []
        for u in range(unroll):
            j = jj * unroll + u
            blocks.append((j, slope * (j * bk - i * bq).astype(jnp.float32), [None] * n_sub))
        run_blocks(blocks)
        return carry

    lax.fori_loop(0, i * (blocks_per_tile // unroll), body, 0)

    kk = lax.broadcasted_iota(jnp.int32, (bk, cw), 0)
    qq = lax.broadcasted_iota(jnp.int32, (bk, cw), 1)
    diag_blocks = []
    for d in range(blocks_per_tile):
        masks = []
        for qs in range(n_sub):
            lo_key, hi_key = d * bk, d * bk + bk - 1
            lo_q, hi_q = qs * cw, qs * cw + cw - 1
            if hi_key <= lo_q:
                masks.append(None)
            elif lo_key > hi_q:
                masks.append(_SKIP)
            else:
                masks.append(kk + lo_key <= qq + lo_q)
        diag_blocks.append((i * blocks_per_tile + d, slope * float(d * bk), masks))
    run_blocks(diag_blocks)

    lam = (jnp.exp(jnp.sum(lq1_ref[...] * lk1_ref[...], axis=1, keepdims=True))
           - jnp.exp(jnp.sum(lq2_ref[...] * lk2_ref[...], axis=1, keepdims=True))
           + lam_init)
    o = acc_ref[:HEAD_W, :] / acc_ref[HEAD_W:HEAD_W + 1, :]
    a = o[:, :bq] - lam * o[:, bq:]
    a = a * lax.rsqrt(jnp.mean(a * a, axis=0, keepdims=True) + NORM_EPS) * subg_ref[...] * (1.0 - lam_init)
    o_ref[...] = a.T.astype(o_ref.dtype)


def _diff_attn(proj, lq1, lk1, lq2, lk2, subg, lam_init):
    b, s, _ = proj.shape
    bq, bk, nh = ATTN_BQ, ATTN_BK, N_DIFF_HEADS
    slopes = jnp.exp2(-8.0 * jnp.arange(1, nh + 1, dtype=jnp.float32) / nh) * LOG2E
    qt = proj[:, :, :DIFF_WIDTH].reshape(b, s, nh, HEAD_W).transpose(0, 2, 3, 1)
    vt = proj[:, :, 2 * DIFF_WIDTH:3 * DIFF_WIDTH].reshape(b, s // bk, bk, nh, HEAD_W)
    vt = vt.transpose(0, 3, 1, 4, 2)
    kern = functools.partial(_diff_attn_kernel, bq=bq, bk=bk, lam_init=lam_init)
    small = lambda n: pl.BlockSpec((1, n), lambda bi, h, i: (0, 0))
    return pl.pallas_call(
        kern,
        grid=(b, nh, s // bq),
        in_specs=[
            pl.BlockSpec(memory_space=pltpu.SMEM),
            small(DH_DIFF), small(DH_DIFF), small(DH_DIFF), small(DH_DIFF),
            pl.BlockSpec((HEAD_W, 1), lambda bi, h, i: (0, 0)),
            pl.BlockSpec((None, None, HEAD_W, bq), lambda bi, h, i: (bi, h, 0, i)),
            pl.BlockSpec((None, s, HEAD_W), lambda bi, h, i: (bi, 0, nh + h)),
            pl.BlockSpec((None, None, s // bk, HEAD_W, bk), lambda bi, h, i: (bi, h, 0, 0, 0)),
        ],
        out_specs=pl.BlockSpec((None, bq, HEAD_W), lambda bi, h, i: (bi, i, h)),
        out_shape=jax.ShapeDtypeStruct((b, s, DIFF_WIDTH), jnp.bfloat16),
        scratch_shapes=[
            pltpu.VMEM((2 * HEAD_W, 2 * bq), jnp.bfloat16),
            pltpu.VMEM((bk, HEAD_W), jnp.bfloat16),
            pltpu.VMEM((1, 2 * bq), jnp.float32),
            pltpu.VMEM((HEAD_W + ACC_PAD, 2 * bq), jnp.float32),
        ],
        compiler_params=pltpu.CompilerParams(
            dimension_semantics=("arbitrary", "arbitrary", "arbitrary"),
            vmem_limit_bytes=VMEM_LIMIT_BYTES),
        name="diff_attn",
    )(slopes, lq1, lk1, lq2, lk2, subg, qt, proj, vt)


def _merge_kernel(x_ref, a_ref, u_ref, uh_ref, qm_ref, g0_ref, g1_ref, g2_ref, km_ref, vm_ref,
                  wa_ref, pw_ref, ps_ref, wp_ref, wm_ref, wo_ref, gpost_ref, o_ref, *, tm):
    i = pl.program_id(1)

    y_attn = _dot(a_ref[...], wa_ref[...])

    u = u_ref[...].astype(jnp.float32)
    halo = jnp.where(i > 0, uh_ref[...].astype(jnp.float32), 0.0)
    t = i * tm + lax.broadcasted_iota(jnp.int32, (tm, POOL_GROUP_WIDTH), 0)
    ys = []
    for g, w in enumerate(POOL_WINDOWS):
        cols = slice(g * POOL_GROUP_WIDTH, (g + 1) * POOL_GROUP_WIDTH)
        ug = u[:, cols]
        ext = jnp.concatenate([halo[:, cols], ug], axis=0)
        wsum = ug
        for d in range(1, w):
            wsum = wsum + ext[POOL_HALO - d:POOL_HALO - d + tm]
        count = jnp.minimum(t + 1, w).astype(jnp.float32)
        pooled = wsum / count - ug
        ys.append(_dot(pooled.astype(jnp.bfloat16), pw_ref[g]))
    y = jnp.concatenate(ys, axis=1) * ps_ref[...]
    y_pool = _dot(y.astype(jnp.bfloat16), wp_ref[...])

    outs = []
    for hh in range(N_MEM_HEADS):
        cols = slice(hh * DH_MEM, (hh + 1) * DH_MEM)
        s = _dot_nt(qm_ref[:, cols], km_ref[:, cols])
        p = jnp.exp2(s - jnp.max(s, axis=1, keepdims=True))
        o = _dot(p.astype(jnp.bfloat16), vm_ref[:, cols])
        outs.append(o / jnp.sum(p, axis=1, keepdims=True))
    y_mem = _dot(jnp.concatenate(outs, axis=1).astype(jnp.bfloat16), wm_ref[...])

    mix = (jax.nn.sigmoid(g0_ref[...].astype(jnp.float32)) * y_attn
           + jax.nn.sigmoid(g1_ref[...].astype(jnp.float32)) * y_pool
           + jax.nn.sigmoid(g2_ref[...].astype(jnp.float32)) * y_mem)
    z = _dot(mix.astype(jnp.bfloat16), wo_ref[...])
    o_ref[...] = x_ref[...] + _rms_norm(z, gpost_ref[...])


def _merge(x, a, proj, km, vm, wa, pw, ps, wp, wm, wo, gpost):
    b, s, d = x.shape
    tm = MERGE_TM
    m = km.shape[1]
    halo_blocks = tm // POOL_HALO
    pool_col = 3 * DIFF_WIDTH // POOL_WIDTH
    qm_col = pool_col + 1
    gate_col = (3 * DIFF_WIDTH + POOL_WIDTH + MEM_WIDTH) // d
    const2 = lambda shape: pl.BlockSpec(shape, lambda bi, i: (0, 0))
    tile = lambda w, c: pl.BlockSpec((None, tm, w), lambda bi, i, c=c: (bi, i, c))
    kern = functools.partial(_merge_kernel, tm=tm)
    return pl.pallas_call(
        kern,
        grid=(b, s // tm),
        in_specs=[
            tile(d, 0),
            tile(DIFF_WIDTH, 0),
            tile(POOL_WIDTH, pool_col),
            pl.BlockSpec((None, POOL_HALO, POOL_WIDTH),
                         lambda bi, i: (bi, jnp.maximum(i * halo_blocks - 1, 0), pool_col)),
            tile(MEM_WIDTH, qm_col),
            tile(d, gate_col), tile(d, gate_col + 1), tile(d, gate_col + 2),
            pl.BlockSpec((None, m, MEM_WIDTH), lambda bi, i: (bi, 0, 0)),
            pl.BlockSpec((None, m, MEM_WIDTH), lambda bi, i: (bi, 0, 0)),
            const2((DIFF_WIDTH, d)),
            pl.BlockSpec(pw.shape, lambda bi, i: (0, 0, 0)),
            const2((1, POOL_WIDTH)),
            const2((POOL_WIDTH, d)),
            const2((MEM_WIDTH, d)),
            const2((d, d)),
            const2((1, d)),
        ],
        out_specs=tile(d, 0),
        out_shape=jax.ShapeDtypeStruct((b, s, d), jnp.float32),
        compiler_params=pltpu.CompilerParams(
            dimension_semantics=("arbitrary", "arbitrary"), vmem_limit_bytes=VMEM_LIMIT_BYTES),
        name="merge",
    )(x, a, proj, proj, proj, proj, proj, proj, km, vm, wa, pw, ps, wp, wm, wo, gpost)


def _conv_ffn_kernel(x_ref, gpre_ref, wup_ref, cw_ref, cb_ref, wdn_ref, gpost_ref, o_ref,
                     carry_ref, act_ref, *, tm, chunk):
    i = pl.program_id(1)

    @pl.when(i == 0)
    def _():
        carry_ref[...] = jnp.zeros(carry_ref.shape, jnp.float32)

    x = x_ref[...]
    h = _rms_norm(x, gpre_ref[...]).astype(jnp.bfloat16)

    def conv(c0):
        up = _dot(h, wup_ref[:, c0:c0 + chunk])
        ext = jnp.concatenate([carry_ref[:, c0:c0 + chunk], up], axis=0)
        carry_ref[:, c0:c0 + chunk] = up[tm - CONV_HALO:]
        w = cw_ref[:, c0:c0 + chunk]
        y = cb_ref[:, c0:c0 + chunk] + w[2:3] * up
        for j in range(CONV_WIDTH - 1):
            off = CONV_HALO - (CONV_WIDTH - 1) + j
            y = y + w[j:j + 1] * ext[off:off + tm]
        return y

    for c0 in range(0, D_FF, chunk):
        gate = conv(c0)
        val = conv(D_FF + c0)
        act_ref[:, c0:c0 + chunk] = (jax.nn.gelu(gate, approximate=True) * val).astype(jnp.bfloat16)

    ff = _dot(act_ref[...], wdn_ref[...])
    o_ref[...] = x + _rms_norm(ff, gpost_ref[...])


def _conv_ffn(x, gpre, wup, cw, cb, wdn, gpost):
    b, s, d = x.shape
    tm = FFN_TM
    const2 = lambda shape: pl.BlockSpec(shape, lambda bi, i: (0, 0))
    tile = pl.BlockSpec((None, tm, d), lambda bi, i: (bi, i, 0))
    kern = functools.partial(_conv_ffn_kernel, tm=tm, chunk=FFN_CHUNK)
    return pl.pallas_call(
        kern,
        grid=(b, s // tm),
        in_specs=[
            tile,
            const2((1, d)),
            const2((d, 2 * D_FF)),
            const2((CONV_WIDTH, 2 * D_FF)),
            const2((1, 2 * D_FF)),
            const2((D_FF, d)),
            const2((1, d)),
        ],
        out_specs=tile,
        out_shape=jax.ShapeDtypeStruct((b, s, d), jnp.float32),
        scratch_shapes=[
            pltpu.VMEM((CONV_HALO, 2 * D_FF), jnp.float32),
            pltpu.VMEM((tm, D_FF), jnp.bfloat16),
        ],
        compiler_params=pltpu.CompilerParams(
            dimension_semantics=("arbitrary", "arbitrary"), vmem_limit_bytes=VMEM_LIMIT_BYTES),
        name="conv_ffn",
    )(x, gpre, wup, cw, cb, wdn, gpost)


def kernel(x, mem, norm_mix_pre, w_in, lambda_q1, lambda_k1, lambda_q2, lambda_k2, subln_g, w_attn_branch, pool_w, pool_scale, w_pool_branch, norm_mem, w_mem_kv, w_mem_branch, w_out, norm_mix_post, norm_ffn_pre, w_up, conv_w, conv_b, w_down, norm_ffn_post):
    b, s, d = x.shape
    depth = w_in.shape[0]
    bf = lambda w: w.astype(jnp.bfloat16)
    for l in range(depth):
        lam_init = 0.8 - 0.6 * math.exp(-0.3 * l)
        proj = _in_proj(x.reshape(b * s, d), norm_mix_pre[l][None], bf(w_in[l])).reshape(b, s, IN_COLS)
        km, vm = _mem_kv(mem, norm_mem[l][None], bf(w_mem_kv[l]))
        a = _diff_attn(proj, lambda_q1[l][None], lambda_k1[l][None], lambda_q2[l][None],
                       lambda_k2[l][None], subln_g[l][:, None], lam_init)
        x = _merge(x, a, proj, km, vm, bf(w_attn_branch[l]), bf(pool_w[l]), pool_scale[l][None],
                   bf(w_pool_branch[l]), bf(w_mem_branch[l]), bf(w_out[l]), norm_mix_post[l][None])
        x = _conv_ffn(x, norm_ffn_pre[l][None], bf(w_up[l]), conv_w[l], conv_b[l][None],
                      bf(w_down[l]), norm_ffn_post[l][None])
    return x
```

```python
import functools
import math

import jax
import jax.numpy as jnp
from jax import lax
from jax.experimental import pallas as pl
from jax.experimental.pallas import tpu as pltpu

D_MODEL = 1024
N_MEM_HEADS = 4
DH_MEM = 128
MEM_WIDTH = N_MEM_HEADS * DH_MEM
N_DIFF_HEADS = 8
DH_DIFF = 64
HEAD_W = 2 * DH_DIFF
DIFF_WIDTH = N_DIFF_HEADS * HEAD_W
POOL_WINDOWS = (2, 4, 8, 16)
POOL_GROUP_WIDTH = 128
POOL_WIDTH = len(POOL_WINDOWS) * POOL_GROUP_WIDTH
POOL_HALO = 16
N_BRANCHES = 3
IN_COLS = 3 * DIFF_WIDTH + POOL_WIDTH + MEM_WIDTH + N_BRANCHES * D_MODEL
D_FF = 2816
CONV_WIDTH = 3
CONV_HALO = 8
NORM_EPS = 1e-6
LOG2E = math.log2(math.e)
NEG_BIG = -1e30

VMEM_LIMIT_BYTES = 56 * 1024 * 1024

PROJ_TM = 1024
PROJ_TN = 1024
ATTN_BQ = 2048
ATTN_BK = 256
ATTN_BLOCKS_PER_ITER = 4
ATTN_SCORES_AHEAD = 4
ATTN_CW = 256
BIAS_ROWS = 16
ACC_PAD = 16
MERGE_TM = 512
FFN_TM = 512
FFN_CHUNK = 256


def _rms_norm(x, g):
    return x * lax.rsqrt(jnp.mean(x * x, axis=-1, keepdims=True) + NORM_EPS) * g


def _dot(a, b):
    return jnp.dot(a, b, preferred_element_type=jnp.float32)


def _dot_nt(a, b):
    return lax.dot_general(a, b, (((1,), (1,)), ((), ())), preferred_element_type=jnp.float32)


def _in_proj_kernel(x_ref, g_ref, w_ref, o_ref, h_ref, *, q_scale, n_q_tiles, first_gate_tile):
    j = pl.program_id(1)

    @pl.when(j == 0)
    def _():
        h_ref[...] = _rms_norm(x_ref[...], g_ref[...]).astype(jnp.bfloat16)

    acc = _dot(h_ref[...], w_ref[...])
    scale = jnp.where(j < n_q_tiles, jnp.float32(q_scale),
                      jnp.where(j >= first_gate_tile, jnp.float32(0.5), jnp.float32(1.0)))
    o_ref[...] = (acc * scale).astype(o_ref.dtype)


def _in_proj(x2d, g, w_bf16):
    t = x2d.shape[0]
    grid = (t // PROJ_TM, IN_COLS // PROJ_TN)
    kern = functools.partial(_in_proj_kernel, q_scale=DH_DIFF ** -0.5 * LOG2E,
                             n_q_tiles=DIFF_WIDTH // PROJ_TN,
                             first_gate_tile=(IN_COLS - N_BRANCHES * D_MODEL) // PROJ_TN)
    return pl.pallas_call(
        kern,
        grid=grid,
        in_specs=[
            pl.BlockSpec((PROJ_TM, D_MODEL), lambda i, j: (i, 0)),
            pl.BlockSpec((1, D_MODEL), lambda i, j: (0, 0)),
            pl.BlockSpec((D_MODEL, PROJ_TN), lambda i, j: (0, j)),
        ],
        out_specs=pl.BlockSpec((PROJ_TM, PROJ_TN), lambda i, j: (i, j)),
        out_shape=jax.ShapeDtypeStruct((t, IN_COLS), jnp.bfloat16),
        scratch_shapes=[pltpu.VMEM((PROJ_TM, D_MODEL), jnp.bfloat16)],
        compiler_params=pltpu.CompilerParams(
            dimension_semantics=("arbitrary", "arbitrary"), vmem_limit_bytes=VMEM_LIMIT_BYTES),
        name="in_proj",
    )(x2d, g, w_bf16)


def _mem_kv_kernel(mem_ref, g_ref, w_ref, k_ref, v_ref, *, k_scale):
    mem_n = _rms_norm(mem_ref[...], g_ref[...]).astype(jnp.bfloat16)
    kv = _dot(mem_n, w_ref[...])
    k_ref[...] = (kv[:, :MEM_WIDTH] * k_scale).astype(k_ref.dtype)
    v_ref[...] = kv[:, MEM_WIDTH:].astype(v_ref.dtype)


def _mem_kv(mem, g, w_bf16):
    b, m, _ = mem.shape
    kern = functools.partial(_mem_kv_kernel, k_scale=DH_MEM ** -0.5 * LOG2E)
    out = jax.ShapeDtypeStruct((b, m, MEM_WIDTH), jnp.bfloat16)
    return pl.pallas_call(
        kern,
        grid=(b,),
        in_specs=[
            pl.BlockSpec((None, m, D_MODEL), lambda i: (i, 0, 0)),
            pl.BlockSpec((1, D_MODEL), lambda i: (0, 0)),
            pl.BlockSpec((D_MODEL, 2 * MEM_WIDTH), lambda i: (0, 0)),
        ],
        out_specs=[pl.BlockSpec((None, m, MEM_WIDTH), lambda i: (i, 0, 0))] * 2,
        out_shape=[out, out],
        compiler_params=pltpu.CompilerParams(
            dimension_semantics=("arbitrary",), vmem_limit_bytes=VMEM_LIMIT_BYTES),
        name="mem_kv",
    )(mem, g, w_bf16)


_SKIP = "skip"


def _diff_attn_kernel(slope_ref, lq1_ref, lk1_ref, lq2_ref, lk2_ref, subg_ref,
                      q_ref, k_ref, v_ref, o_ref,
                      qbd_ref, kfeat_ref, m_ref, acc_ref, *, bq, bk, lam_init):
    h = pl.program_id(1)
    i = pl.program_id(2)
    cw = ATTN_CW
    n_sub = bq // cw
    slope = slope_ref[h]

    def split3(x):
        a = x.astype(jnp.bfloat16).astype(jnp.float32)
        b = (x - a).astype(jnp.bfloat16).astype(jnp.float32)
        return a, b, x - a - b

    qt = q_ref[...].T
    feat = lax.broadcasted_iota(jnp.int32, qt.shape, 0)
    zero = jnp.zeros_like(qt)
    qbd_ref[:HEAD_W, :bq] = jnp.where(feat < DH_DIFF, qt, zero)
    qbd_ref[:HEAD_W, bq:] = jnp.where(feat >= DH_DIFF, qt, zero)
    r = lax.broadcasted_iota(jnp.int32, (BIAS_ROWS, bq), 0)
    ii = lax.broadcasted_iota(jnp.int32, (BIAS_ROWS, bq), 1).astype(jnp.float32)
    pieces = split3(jnp.full((BIAS_ROWS, bq), slope, jnp.float32)) + split3(-ii * slope)
    rows = jnp.zeros((BIAS_ROWS, bq), jnp.float32)
    for n, piece in enumerate(pieces):
        rows = jnp.where(r == n, piece, rows)
    rows = rows.astype(jnp.bfloat16)
    qbd_ref[HEAD_W:HEAD_W + BIAS_ROWS, :bq] = rows
    qbd_ref[HEAD_W:HEAD_W + BIAS_ROWS, bq:] = rows
    qbd_ref[HEAD_W + BIAS_ROWS:, :] = jnp.zeros((HEAD_W - BIAS_ROWS, 2 * bq), jnp.bfloat16)

    jj = lax.broadcasted_iota(jnp.int32, (bk, HEAD_W), 0).astype(jnp.float32)
    fcol = lax.broadcasted_iota(jnp.int32, (bk, HEAD_W), 1)
    kfeat_ref[...] = jnp.where(fcol < 3, jj, jnp.where(fcol < 6, 1.0, 0.0)).astype(jnp.bfloat16)

    m_ref[...] = jnp.full(m_ref.shape, NEG_BIG, jnp.float32)
    acc_ref[...] = jnp.zeros(acc_ref.shape, jnp.float32)
    orow = lax.broadcasted_iota(jnp.int32, (ACC_PAD, bk), 0)
    ones_rows = jnp.where(orow == 0, 1.0, 0.0).astype(jnp.bfloat16)

    def run_blocks(blocks):
        steps = []
        for blk, c, masks in blocks:
            keys = pl.ds(pl.multiple_of(blk * bk, bk), bk)
            kb = jnp.concatenate([k_ref[keys, :], kfeat_ref[...]], axis=1)
            vtb = jnp.concatenate([v_ref[keys, :].T, ones_rows], axis=0)
            steps += [(kb, vtb, c, ch, masks[ch % n_sub])
                      for ch in range(2 * n_sub) if masks[ch % n_sub] is not _SKIP]

        def scores(step):
            kb, _, _, ch, mask = step
            s = _dot(kb, qbd_ref[:, ch * cw:(ch + 1) * cw])
            return s if mask is None else jnp.where(mask, s, NEG_BIG)

        def update(step, s):
            _, vtb, c, ch, _ = step
            lanes = slice(ch * cw, (ch + 1) * cw)
            m_prev = m_ref[:, lanes]
            m_next = jnp.maximum(m_prev, jnp.max(s, axis=0, keepdims=True) + c)
            p = jnp.exp2(s - (m_next - c))
            alpha = jnp.exp2(m_prev - m_next)
            m_ref[:, lanes] = m_next
            acc_ref[:, lanes] = alpha * acc_ref[:, lanes] + _dot(vtb, p.astype(jnp.bfloat16))

        ahead = ATTN_SCORES_AHEAD
        pending = [scores(step) for step in steps[:ahead]]
        for n, step in enumerate(steps):
            if n + ahead < len(steps):
                pending.append(scores(steps[n + ahead]))
            update(step, pending.pop(0))

    blocks_per_tile = bq // bk
    unroll = ATTN_BLOCKS_PER_ITER

    def body(jj, carry):
        blocks = []
        for u in range(unroll):
            j = jj * unroll + u
            blocks.append((j, slope * (j * bk - i * bq).astype(jnp.float32), [None] * n_sub))
        run_blocks(blocks)
        return carry

    lax.fori_loop(0, i * (blocks_per_tile // unroll), body, 0)

    kk = lax.broadcasted_iota(jnp.int32, (bk, cw), 0)
    qq = lax.broadcasted_iota(jnp.int32, (bk, cw), 1)
    diag_blocks = []
    for d in range(blocks_per_tile):
        masks = []
        for qs in range(n_sub):
            lo_key, hi_key = d * bk, d * bk + bk - 1
            lo_q, hi_q = qs * cw, qs * cw + cw - 1
            if hi_key <= lo_q:
                masks.append(None)
            elif lo_key > hi_q:
                masks.append(_SKIP)
            else:
                masks.append(kk + lo_key <= qq + lo_q)
        diag_blocks.append((i * blocks_per_tile + d, slope * float(d * bk), masks))
    run_blocks(diag_blocks)

    lam = (jnp.exp(jnp.sum(lq1_ref[...] * lk1_ref[...], axis=1, keepdims=True))
           - jnp.exp(jnp.sum(lq2_ref[...] * lk2_ref[...], axis=1, keepdims=True))
           + lam_init)
    o = acc_ref[:HEAD_W, :] / acc_ref[HEAD_W:HEAD_W + 1, :]
    a = o[:, :bq] - lam * o[:, bq:]
    a = a * lax.rsqrt(jnp.mean(a * a, axis=0, keepdims=True) + NORM_EPS) * subg_ref[...] * (1.0 - lam_init)
    o_ref[...] = a.T.astype(o_ref.dtype)


def _diff_attn(proj, lq1, lk1, lq2, lk2, subg, lam_init):
    b, s, _ = proj.shape
    bq, bk, nh = ATTN_BQ, ATTN_BK, N_DIFF_HEADS
    slopes = jnp.exp2(-8.0 * jnp.arange(1, nh + 1, dtype=jnp.float32) / nh) * LOG2E
    kern = functools.partial(_diff_attn_kernel, bq=bq, bk=bk, lam_init=lam_init)
    small = lambda n: pl.BlockSpec((1, n), lambda bi, h, i: (0, 0))
    return pl.pallas_call(
        kern,
        grid=(b, nh, s // bq),
        in_specs=[
            pl.BlockSpec(memory_space=pltpu.SMEM),
            small(DH_DIFF), small(DH_DIFF), small(DH_DIFF), small(DH_DIFF),
            pl.BlockSpec((HEAD_W, 1), lambda bi, h, i: (0, 0)),
            pl.BlockSpec((None, bq, HEAD_W), lambda bi, h, i: (bi, i, h)),
            pl.BlockSpec((None, s, HEAD_W), lambda bi, h, i: (bi, 0, nh + h)),
            pl.BlockSpec((None, s, HEAD_W), lambda bi, h, i: (bi, 0, 2 * nh + h)),
        ],
        out_specs=pl.BlockSpec((None, bq, HEAD_W), lambda bi, h, i: (bi, i, h)),
        out_shape=jax.ShapeDtypeStruct((b, s, DIFF_WIDTH), jnp.bfloat16),
        scratch_shapes=[
            pltpu.VMEM((2 * HEAD_W, 2 * bq), jnp.bfloat16),
            pltpu.VMEM((bk, HEAD_W), jnp.bfloat16),
            pltpu.VMEM((1, 2 * bq), jnp.float32),
            pltpu.VMEM((HEAD_W + ACC_PAD, 2 * bq), jnp.float32),
        ],
        compiler_params=pltpu.CompilerParams(
            dimension_semantics=("arbitrary", "arbitrary", "arbitrary"),
            vmem_limit_bytes=VMEM_LIMIT_BYTES),
        name="diff_attn",
    )(slopes, lq1, lk1, lq2, lk2, subg, proj, proj, proj)


def _merge_kernel(x_ref, a_ref, u_ref, uh_ref, qm_ref, g0_ref, g1_ref, g2_ref, km_ref, vm_ref,
                  wa_ref, pw_ref, ps_ref, wp_ref, wm_ref, wo_ref, gpost_ref, o_ref, *, tm):
    i = pl.program_id(1)

    y_attn = _dot(a_ref[...], wa_ref[...])

    u = u_ref[...].astype(jnp.float32)
    halo = jnp.where(i > 0, uh_ref[...].astype(jnp.float32), 0.0)
    t = i * tm + lax.broadcasted_iota(jnp.int32, (tm, POOL_GROUP_WIDTH), 0)
    wsum = jnp.concatenate([halo, u], axis=0)
    ys = []
    for g, w in enumerate(POOL_WINDOWS):
        wsum = wsum + pltpu.roll(wsum, w // 2, axis=0)
        ug = u[:, g * POOL_GROUP_WIDTH:(g + 1) * POOL_GROUP_WIDTH]
        count = jnp.minimum(t + 1, w).astype(jnp.float32)
        pooled = wsum[POOL_HALO:, :POOL_GROUP_WIDTH] / count - ug
        wsum = wsum[:, POOL_GROUP_WIDTH:]
        ys.append(_dot(pooled.astype(jnp.bfloat16), pw_ref[g]))
    y = jnp.concatenate(ys, axis=1) * ps_ref[...]
    y_pool = _dot(y.astype(jnp.bfloat16), wp_ref[...])

    outs = []
    for hh in range(N_MEM_HEADS):
        cols = slice(hh * DH_MEM, (hh + 1) * DH_MEM)
        s = _dot_nt(qm_ref[:, cols], km_ref[:, cols])
        p = jnp.exp2(s - jnp.max(s, axis=1, keepdims=True))
        o = _dot(p.astype(jnp.bfloat16), vm_ref[:, cols])
        outs.append(o / jnp.sum(p, axis=1, keepdims=True))
    y_mem = _dot(jnp.concatenate(outs, axis=1).astype(jnp.bfloat16), wm_ref[...])

    mix = y_attn + y_attn * jnp.tanh(g0_ref[...].astype(jnp.float32))
    mix = mix + y_pool + y_pool * jnp.tanh(g1_ref[...].astype(jnp.float32))
    mix = mix + y_mem + y_mem * jnp.tanh(g2_ref[...].astype(jnp.float32))
    z = _dot((0.5 * mix).astype(jnp.bfloat16), wo_ref[...])
    o_ref[...] = x_ref[...] + _rms_norm(z, gpost_ref[...])


def _merge(x, a, proj, km, vm, wa, pw, ps, wp, wm, wo, gpost):
    b, s, d = x.shape
    tm = MERGE_TM
    m = km.shape[1]
    halo_blocks = tm // POOL_HALO
    pool_col = 3 * DIFF_WIDTH // POOL_WIDTH
    qm_col = pool_col + 1
    gate_col = (3 * DIFF_WIDTH + POOL_WIDTH + MEM_WIDTH) // d
    const2 = lambda shape: pl.BlockSpec(shape, lambda bi, i: (0, 0), pipeline_mode=pl.Buffered(1))
    tile = lambda w, c: pl.BlockSpec((None, tm, w), lambda bi, i, c=c: (bi, i, c))
    kern = functools.partial(_merge_kernel, tm=tm)
    return pl.pallas_call(
        kern,
        grid=(b, s // tm),
        in_specs=[
            tile(d, 0),
            tile(DIFF_WIDTH, 0),
            tile(POOL_WIDTH, pool_col),
            pl.BlockSpec((None, POOL_HALO, POOL_WIDTH),
                         lambda bi, i: (bi, jnp.maximum(i * halo_blocks - 1, 0), pool_col)),
            tile(MEM_WIDTH, qm_col),
            tile(d, gate_col), tile(d, gate_col + 1), tile(d, gate_col + 2),
            pl.BlockSpec((None, m, MEM_WIDTH), lambda bi, i: (bi, 0, 0)),
            pl.BlockSpec((None, m, MEM_WIDTH), lambda bi, i: (bi, 0, 0)),
            const2((DIFF_WIDTH, d)),
            pl.BlockSpec(pw.shape, lambda bi, i: (0, 0, 0)),
            const2((1, POOL_WIDTH)),
            const2((POOL_WIDTH, d)),
            const2((MEM_WIDTH, d)),
            const2((d, d)),
            const2((1, d)),
        ],
        out_specs=tile(d, 0),
        out_shape=jax.ShapeDtypeStruct((b, s, d), jnp.float32),
        compiler_params=pltpu.CompilerParams(
            dimension_semantics=("arbitrary", "arbitrary"), vmem_limit_bytes=VMEM_LIMIT_BYTES),
        name="merge",
    )(x, a, proj, proj, proj, proj, proj, proj, km, vm, wa, pw, ps, wp, wm, wo, gpost)


def _conv_ffn_kernel(x_ref, gpre_ref, wup_ref, cw_ref, cb_ref, wdn_ref, gpost_ref, o_ref,
                     carry_ref, act_ref, *, tm, chunk):
    i = pl.program_id(1)

    @pl.when(i == 0)
    def _():
        carry_ref[...] = jnp.zeros(carry_ref.shape, jnp.float32)

    x = x_ref[...]
    h = _rms_norm(x, gpre_ref[...]).astype(jnp.bfloat16)

    def conv(c0):
        up = _dot(h, wup_ref[:, c0:c0 + chunk])
        ext = jnp.concatenate([carry_ref[:, c0:c0 + chunk], up], axis=0)
        carry_ref[:, c0:c0 + chunk] = up[tm - CONV_HALO:]
        w = cw_ref[:, c0:c0 + chunk]
        y = cb_ref[:, c0:c0 + chunk] + w[CONV_WIDTH - 1:CONV_WIDTH] * up
        for j in range(CONV_WIDTH - 1):
            off = CONV_HALO - (CONV_WIDTH - 1) + j
            y = y + w[j:j + 1] * ext[off:off + tm]
        return y

    for c0 in range(0, D_FF, chunk):
        gate = conv(c0)
        val = conv(D_FF + c0)
        act_ref[:, c0:c0 + chunk] = (jax.nn.gelu(gate, approximate=True) * val).astype(jnp.bfloat16)

    ff = _dot(act_ref[...], wdn_ref[...])
    o_ref[...] = x + _rms_norm(ff, gpost_ref[...])


def _conv_ffn(x, gpre, wup, cw, cb, wdn, gpost):
    b, s, d = x.shape
    tm = FFN_TM
    const2 = lambda shape: pl.BlockSpec(shape, lambda bi, i: (0, 0), pipeline_mode=pl.Buffered(1))
    tile = pl.BlockSpec((None, tm, d), lambda bi, i: (bi, i, 0))
    kern = functools.partial(_conv_ffn_kernel, tm=tm, chunk=FFN_CHUNK)
    return pl.pallas_call(
        kern,
        grid=(b, s // tm),
        in_specs=[
            tile,
            const2((1, d)),
            const2((d, 2 * D_FF)),
            const2((CONV_WIDTH, 2 * D_FF)),
            const2((1, 2 * D_FF)),
            const2((D_FF, d)),
            const2((1, d)),
        ],
        out_specs=tile,
        out_shape=jax.ShapeDtypeStruct((b, s, d), jnp.float32),
        scratch_shapes=[
            pltpu.VMEM((CONV_HALO, 2 * D_FF), jnp.float32),
            pltpu.VMEM((tm, D_FF), jnp.bfloat16),
        ],
        compiler_params=pltpu.CompilerParams(
            dimension_semantics=("arbitrary", "arbitrary"), vmem_limit_bytes=VMEM_LIMIT_BYTES),
        name="conv_ffn",
    )(x, gpre, wup, cw, cb, wdn, gpost)


def kernel(x, mem, norm_mix_pre, w_in, lambda_q1, lambda_k1, lambda_q2, lambda_k2, subln_g, w_attn_branch, pool_w, pool_scale, w_pool_branch, norm_mem, w_mem_kv, w_mem_branch, w_out, norm_mix_post, norm_ffn_pre, w_up, conv_w, conv_b, w_down, norm_ffn_post):
    b, s, d = x.shape
    depth = w_in.shape[0]
    bf = lambda w: w.astype(jnp.bfloat16)
    for l in range(depth):
        lam_init = 0.8 - 0.6 * math.exp(-0.3 * l)
        proj = _in_proj(x.reshape(b * s, d), norm_mix_pre[l][None], bf(w_in[l])).reshape(b, s, IN_COLS)
        km, vm = _mem_kv(mem, norm_mem[l][None], bf(w_mem_kv[l]))
        a = _diff_attn(proj, lambda_q1[l][None], lambda_k1[l][None], lambda_q2[l][None],
                       lambda_k2[l][None], subln_g[l][:, None], lam_init)
        x = _merge(x, a, proj, km, vm, bf(w_attn_branch[l]), bf(pool_w[l]), pool_scale[l][None],
                   bf(w_pool_branch[l]), bf(w_mem_branch[l]), bf(w_out[l]), norm_mix_post[l][None])
        x = _conv_ffn(x, norm_ffn_pre[l][None], bf(w_up[l]), conv_w[l], conv_b[l][None],
                      bf(w_down[l]), norm_ffn_post[l][None])
    return x
```

```python
import functools
import math

import jax
import jax.numpy as jnp
from jax import lax
from jax.experimental import pallas as pl
from jax.experimental.pallas import tpu as pltpu

D_MODEL = 1024
N_MEM_HEADS = 4
DH_MEM = 128
MEM_WIDTH = N_MEM_HEADS * DH_MEM
N_DIFF_HEADS = 8
DH_DIFF = 64
HEAD_W = 2 * DH_DIFF
DIFF_WIDTH = N_DIFF_HEADS * HEAD_W
POOL_WINDOWS = (2, 4, 8, 16)
POOL_GROUP_WIDTH = 128
POOL_WIDTH = len(POOL_WINDOWS) * POOL_GROUP_WIDTH
POOL_HALO = 16
N_BRANCHES = 3
IN_COLS = 3 * DIFF_WIDTH + POOL_WIDTH + MEM_WIDTH + N_BRANCHES * D_MODEL
D_FF = 2816
CONV_WIDTH = 3
CONV_HALO = 8
NORM_EPS = 1e-6
LOG2E = math.log2(math.e)
NEG_BIG = -1e30

VMEM_LIMIT_BYTES = 56 * 1024 * 1024

PROJ_TM = 1024
PROJ_TN = 1024
ATTN_BQ = 2048
ATTN_BK = 256
ATTN_BLOCKS_PER_ITER = 4
ATTN_SCORES_AHEAD = 4
ATTN_CW = 256
BIAS_ROWS = 16
ACC_PAD = 16
MERGE_TM = 512
FFN_TM = 512
FFN_CHUNK = 256


def _rms_norm(x, g):
    return x * lax.rsqrt(jnp.mean(x * x, axis=-1, keepdims=True) + NORM_EPS) * g


def _dot(a, b):
    return jnp.dot(a, b, preferred_element_type=jnp.float32)


def _dot_nt(a, b):
    return lax.dot_general(a, b, (((1,), (1,)), ((), ())), preferred_element_type=jnp.float32)


def _in_proj_kernel(x_ref, g_ref, w_ref, o_ref, h_ref, *, q_scale, n_q_tiles, first_gate_tile):
    j = pl.program_id(1)

    @pl.when(j == 0)
    def _():
        h_ref[...] = _rms_norm(x_ref[...], g_ref[...]).astype(jnp.bfloat16)

    acc = _dot(h_ref[...], w_ref[...])
    scale = jnp.where(j < n_q_tiles, jnp.float32(q_scale),
                      jnp.where(j >= first_gate_tile, jnp.float32(0.5), jnp.float32(1.0)))
    o_ref[...] = (acc * scale).astype(o_ref.dtype)


def _proj_block(col, width):
    assert col % width == 0 and PROJ_TN % width == 0
    return col // PROJ_TN, (col % PROJ_TN) // width


def _in_proj(x2d, g, w_tiles):
    t = x2d.shape[0]
    grid = (t // PROJ_TM, IN_COLS // PROJ_TN)
    kern = functools.partial(_in_proj_kernel, q_scale=DH_DIFF ** -0.5 * LOG2E,
                             n_q_tiles=DIFF_WIDTH // PROJ_TN,
                             first_gate_tile=(IN_COLS - N_BRANCHES * D_MODEL) // PROJ_TN)
    return pl.pallas_call(
        kern,
        grid=grid,
        in_specs=[
            pl.BlockSpec((PROJ_TM, D_MODEL), lambda i, j: (i, 0)),
            pl.BlockSpec((1, D_MODEL), lambda i, j: (0, 0)),
            pl.BlockSpec((None, D_MODEL, PROJ_TN), lambda i, j: (j, 0, 0)),
        ],
        out_specs=pl.BlockSpec((None, PROJ_TM, PROJ_TN), lambda i, j: (j, i, 0)),
        out_shape=jax.ShapeDtypeStruct((IN_COLS // PROJ_TN, t, PROJ_TN), jnp.bfloat16),
        scratch_shapes=[pltpu.VMEM((PROJ_TM, D_MODEL), jnp.bfloat16)],
        compiler_params=pltpu.CompilerParams(
            dimension_semantics=("arbitrary", "arbitrary"), vmem_limit_bytes=VMEM_LIMIT_BYTES),
        name="in_proj",
    )(x2d, g, w_tiles)


def _mem_kv_kernel(mem_ref, g_ref, w_ref, k_ref, v_ref, *, k_scale):
    mem_n = _rms_norm(mem_ref[...], g_ref[...]).astype(jnp.bfloat16)
    kv = _dot(mem_n, w_ref[...])
    k_ref[...] = (kv[:, :MEM_WIDTH] * k_scale).astype(k_ref.dtype)
    v_ref[...] = kv[:, MEM_WIDTH:].astype(v_ref.dtype)


def _mem_kv(mem, g, w_bf16):
    b, m, _ = mem.shape
    kern = functools.partial(_mem_kv_kernel, k_scale=DH_MEM ** -0.5 * LOG2E)
    out = jax.ShapeDtypeStruct((b, m, MEM_WIDTH), jnp.bfloat16)
    return pl.pallas_call(
        kern,
        grid=(b,),
        in_specs=[
            pl.BlockSpec((None, m, D_MODEL), lambda i: (i, 0, 0)),
            pl.BlockSpec((1, D_MODEL), lambda i: (0, 0)),
            pl.BlockSpec((D_MODEL, 2 * MEM_WIDTH), lambda i: (0, 0)),
        ],
        out_specs=[pl.BlockSpec((None, m, MEM_WIDTH), lambda i: (i, 0, 0))] * 2,
        out_shape=[out, out],
        compiler_params=pltpu.CompilerParams(
            dimension_semantics=("arbitrary",), vmem_limit_bytes=VMEM_LIMIT_BYTES),
        name="mem_kv",
    )(mem, g, w_bf16)


_SKIP = "skip"


def _diff_attn_kernel(slope_ref, lq1_ref, lk1_ref, lq2_ref, lk2_ref, subg_ref,
                      q_ref, k_ref, v_ref, o_ref,
                      qbd_ref, kfeat_ref, m_ref, acc_ref, *, bq, bk, lam_init):
    h = pl.program_id(1)
    i = pl.program_id(2)
    cw = ATTN_CW
    n_sub = bq // cw
    slope = slope_ref[h]

    def split3(x):
        a = x.astype(jnp.bfloat16).astype(jnp.float32)
        b = (x - a).astype(jnp.bfloat16).astype(jnp.float32)
        return a, b, x - a - b

    qt = q_ref[...].T
    feat = lax.broadcasted_iota(jnp.int32, qt.shape, 0)
    zero = jnp.zeros_like(qt)
    qbd_ref[:HEAD_W, :bq] = jnp.where(feat < DH_DIFF, qt, zero)
    qbd_ref[:HEAD_W, bq:] = jnp.where(feat >= DH_DIFF, qt, zero)
    r = lax.broadcasted_iota(jnp.int32, (BIAS_ROWS, bq), 0)
    ii = lax.broadcasted_iota(jnp.int32, (BIAS_ROWS, bq), 1).astype(jnp.float32)
    pieces = split3(jnp.full((BIAS_ROWS, bq), slope, jnp.float32)) + split3(-ii * slope)
    rows = jnp.zeros((BIAS_ROWS, bq), jnp.float32)
    for n, piece in enumerate(pieces):
        rows = jnp.where(r == n, piece, rows)
    rows = rows.astype(jnp.bfloat16)
    qbd_ref[HEAD_W:HEAD_W + BIAS_ROWS, :bq] = rows
    qbd_ref[HEAD_W:HEAD_W + BIAS_ROWS, bq:] = rows
    qbd_ref[HEAD_W + BIAS_ROWS:, :] = jnp.zeros((HEAD_W - BIAS_ROWS, 2 * bq), jnp.bfloat16)

    jj = lax.broadcasted_iota(jnp.int32, (bk, HEAD_W), 0).astype(jnp.float32)
    fcol = lax.broadcasted_iota(jnp.int32, (bk, HEAD_W), 1)
    kfeat_ref[...] = jnp.where(fcol < 3, jj, jnp.where(fcol < 6, 1.0, 0.0)).astype(jnp.bfloat16)

    m_ref[...] = jnp.full(m_ref.shape, NEG_BIG, jnp.float32)
    acc_ref[...] = jnp.zeros(acc_ref.shape, jnp.float32)
    orow = lax.broadcasted_iota(jnp.int32, (ACC_PAD, bk), 0)
    ones_rows = jnp.where(orow == 0, 1.0, 0.0).astype(jnp.bfloat16)

    def run_blocks(blocks):
        steps = []
        for blk, c, masks in blocks:
            keys = pl.ds(pl.multiple_of(blk * bk, bk), bk)
            kb = jnp.concatenate([k_ref[keys, :], kfeat_ref[...]], axis=1)
            vtb = jnp.concatenate([v_ref[keys, :].T, ones_rows], axis=0)
            steps += [(kb, vtb, c, ch, masks[ch % n_sub])
                      for ch in range(2 * n_sub) if masks[ch % n_sub] is not _SKIP]

        def scores(step):
            kb, _, _, ch, mask = step
            s = _dot(kb, qbd_ref[:, ch * cw:(ch + 1) * cw])
            return s if mask is None else jnp.where(mask, s, NEG_BIG)

        def update(step, s):
            _, vtb, c, ch, _ = step
            lanes = slice(ch * cw, (ch + 1) * cw)
            m_prev = m_ref[:, lanes]
            m_next = jnp.maximum(m_prev, jnp.max(s, axis=0, keepdims=True) + c)
            p = jnp.exp2(s - (m_next - c))
            alpha = jnp.exp2(m_prev - m_next)
            m_ref[:, lanes] = m_next
            acc_ref[:, lanes] = alpha * acc_ref[:, lanes] + _dot(vtb, p.astype(jnp.bfloat16))

        ahead = ATTN_SCORES_AHEAD
        pending = [scores(step) for step in steps[:ahead]]
        for n, step in enumerate(steps):
            if n + ahead < len(steps):
                pending.append(scores(steps[n + ahead]))
            update(step, pending.pop(0))

    blocks_per_tile = bq // bk
    unroll = ATTN_BLOCKS_PER_ITER

    def body(jj, carry):
        blocks = []
        for u in range(unroll):
            j = jj * unroll + u
            blocks.append((j, slope * (j * bk - i * bq).astype(jnp.float32), [None] * n_sub))
        run_blocks(blocks)
        return carry

    lax.fori_loop(0, i * (blocks_per_tile // unroll), body, 0)

    kk = lax.broadcasted_iota(jnp.int32, (bk, cw), 0)
    qq = lax.broadcasted_iota(jnp.int32, (bk, cw), 1)
    diag_blocks = []
    for d in range(blocks_per_tile):
        masks = []
        for qs in range(n_sub):
            lo_key, hi_key = d * bk, d * bk + bk - 1
            lo_q, hi_q = qs * cw, qs * cw + cw - 1
            if hi_key <= lo_q:
                masks.append(None)
            elif lo_key > hi_q:
                masks.append(_SKIP)
            else:
                masks.append(kk + lo_key <= qq + lo_q)
        diag_blocks.append((i * blocks_per_tile + d, slope * float(d * bk), masks))
    run_blocks(diag_blocks)

    lam = (jnp.exp(jnp.sum(lq1_ref[...] * lk1_ref[...], axis=1, keepdims=True))
           - jnp.exp(jnp.sum(lq2_ref[...] * lk2_ref[...], axis=1, keepdims=True))
           + lam_init)
    o = acc_ref[:HEAD_W, :] / acc_ref[HEAD_W:HEAD_W + 1, :]
    a = o[:, :bq] - lam * o[:, bq:]
    a = a * lax.rsqrt(jnp.mean(a * a, axis=0, keepdims=True) + NORM_EPS) * subg_ref[...] * (1.0 - lam_init)
    o_ref[...] = a.T.astype(o_ref.dtype)


def _diff_attn(proj, lq1, lk1, lq2, lk2, subg, lam_init):
    _, b, s, _ = proj.shape
    bq, bk, nh = ATTN_BQ, ATTN_BK, N_DIFF_HEADS
    slopes = jnp.exp2(-8.0 * jnp.arange(1, nh + 1, dtype=jnp.float32) / nh) * LOG2E
    kern = functools.partial(_diff_attn_kernel, bq=bq, bk=bk, lam_init=lam_init)
    small = lambda n: pl.BlockSpec((1, n), lambda bi, h, i: (0, 0))
    (qt, q0), (kt, k0), (vt, v0) = (_proj_block(n * DIFF_WIDTH, HEAD_W) for n in range(3))
    return pl.pallas_call(
        kern,
        grid=(b, nh, s // bq),
        in_specs=[
            pl.BlockSpec(memory_space=pltpu.SMEM),
            small(DH_DIFF), small(DH_DIFF), small(DH_DIFF), small(DH_DIFF),
            pl.BlockSpec((HEAD_W, 1), lambda bi, h, i: (0, 0)),
            pl.BlockSpec((None, None, bq, HEAD_W), lambda bi, h, i: (qt, bi, i, q0 + h)),
            pl.BlockSpec((None, None, s, HEAD_W), lambda bi, h, i: (kt, bi, 0, k0 + h)),
            pl.BlockSpec((None, None, s, HEAD_W), lambda bi, h, i: (vt, bi, 0, v0 + h)),
        ],
        out_specs=pl.BlockSpec((None, bq, HEAD_W), lambda bi, h, i: (bi, i, h)),
        out_shape=jax.ShapeDtypeStruct((b, s, DIFF_WIDTH), jnp.bfloat16),
        scratch_shapes=[
            pltpu.VMEM((2 * HEAD_W, 2 * bq), jnp.bfloat16),
            pltpu.VMEM((bk, HEAD_W), jnp.bfloat16),
            pltpu.VMEM((1, 2 * bq), jnp.float32),
            pltpu.VMEM((HEAD_W + ACC_PAD, 2 * bq), jnp.float32),
        ],
        compiler_params=pltpu.CompilerParams(
            dimension_semantics=("arbitrary", "arbitrary", "arbitrary"),
            vmem_limit_bytes=VMEM_LIMIT_BYTES),
        name="diff_attn",
    )(slopes, lq1, lk1, lq2, lk2, subg, proj, proj, proj)


def _merge_kernel(x_ref, a_ref, u_ref, uh_ref, qm_ref, g0_ref, g1_ref, g2_ref, km_ref, vm_ref,
                  wa_ref, pw_ref, ps_ref, wp_ref, wm_ref, wo_ref, gpost_ref, o_ref, *, tm):
    i = pl.program_id(1)

    y_attn = _dot(a_ref[...], wa_ref[...])

    u = u_ref[...].astype(jnp.float32)
    halo = jnp.where(i > 0, uh_ref[...].astype(jnp.float32), 0.0)
    t = i * tm + lax.broadcasted_iota(jnp.int32, (tm, POOL_GROUP_WIDTH), 0)
    wsum = jnp.concatenate([halo, u], axis=0)
    ys = []
    for g, w in enumerate(POOL_WINDOWS):
        wsum = wsum + pltpu.roll(wsum, w // 2, axis=0)
        ug = u[:, g * POOL_GROUP_WIDTH:(g + 1) * POOL_GROUP_WIDTH]
        count = jnp.minimum(t + 1, w).astype(jnp.float32)
        pooled = wsum[POOL_HALO:, :POOL_GROUP_WIDTH] / count - ug
        wsum = wsum[:, POOL_GROUP_WIDTH:]
        ys.append(_dot(pooled.astype(jnp.bfloat16), pw_ref[g]))
    y = jnp.concatenate(ys, axis=1) * ps_ref[...]
    y_pool = _dot(y.astype(jnp.bfloat16), wp_ref[...])

    outs = []
    for hh in range(N_MEM_HEADS):
        cols = slice(hh * DH_MEM, (hh + 1) * DH_MEM)
        s = _dot_nt(qm_ref[:, cols], km_ref[:, cols])
        p = jnp.exp2(s - jnp.max(s, axis=1, keepdims=True))
        o = _dot(p.astype(jnp.bfloat16), vm_ref[:, cols])
        outs.append(o / jnp.sum(p, axis=1, keepdims=True))
    y_mem = _dot(jnp.concatenate(outs, axis=1).astype(jnp.bfloat16), wm_ref[...])

    mix = y_attn + y_attn * jnp.tanh(g0_ref[...].astype(jnp.float32))
    mix = mix + y_pool + y_pool * jnp.tanh(g1_ref[...].astype(jnp.float32))
    mix = mix + y_mem + y_mem * jnp.tanh(g2_ref[...].astype(jnp.float32))
    z = _dot((0.5 * mix).astype(jnp.bfloat16), wo_ref[...])
    o_ref[...] = x_ref[...] + _rms_norm(z, gpost_ref[...])


def _merge(x, a, proj, km, vm, wa, pw, ps, wp, wm, wo, gpost):
    b, s, d = x.shape
    tm = MERGE_TM
    m = km.shape[1]
    halo_blocks = tm // POOL_HALO
    pool_t, pool_c = _proj_block(3 * DIFF_WIDTH, POOL_WIDTH)
    qm_t, qm_c = _proj_block(3 * DIFF_WIDTH + POOL_WIDTH, MEM_WIDTH)
    gate_col = 3 * DIFF_WIDTH + POOL_WIDTH + MEM_WIDTH
    const2 = lambda shape: pl.BlockSpec(shape, lambda bi, i: (0, 0), pipeline_mode=pl.Buffered(1))
    tile = lambda w, c: pl.BlockSpec((None, tm, w), lambda bi, i, c=c: (bi, i, c))

    def ptile(w, tc):
        t, c = tc
        return pl.BlockSpec((None, None, tm, w), lambda bi, i: (t, bi, i, c))

    kern = functools.partial(_merge_kernel, tm=tm)
    return pl.pallas_call(
        kern,
        grid=(b, s // tm),
        in_specs=[
            tile(d, 0),
            tile(DIFF_WIDTH, 0),
            ptile(POOL_WIDTH, (pool_t, pool_c)),
            pl.BlockSpec((None, None, POOL_HALO, POOL_WIDTH),
                         lambda bi, i: (pool_t, bi, jnp.maximum(i * halo_blocks - 1, 0), pool_c)),
            ptile(MEM_WIDTH, (qm_t, qm_c)),
            ptile(d, _proj_block(gate_col, d)), ptile(d, _proj_block(gate_col + d, d)),
            ptile(d, _proj_block(gate_col + 2 * d, d)),
            pl.BlockSpec((None, m, MEM_WIDTH), lambda bi, i: (bi, 0, 0)),
            pl.BlockSpec((None, m, MEM_WIDTH), lambda bi, i: (bi, 0, 0)),
            const2((DIFF_WIDTH, d)),
            pl.BlockSpec(pw.shape, lambda bi, i: (0, 0, 0)),
            const2((1, POOL_WIDTH)),
            const2((POOL_WIDTH, d)),
            const2((MEM_WIDTH, d)),
            const2((d, d)),
            const2((1, d)),
        ],
        out_specs=tile(d, 0),
        out_shape=jax.ShapeDtypeStruct((b, s, d), jnp.float32),
        compiler_params=pltpu.CompilerParams(
            dimension_semantics=("arbitrary", "arbitrary"), vmem_limit_bytes=VMEM_LIMIT_BYTES),
        name="merge",
    )(x, a, proj, proj, proj, proj, proj, proj, km, vm, wa, pw, ps, wp, wm, wo, gpost)


def _conv_ffn_kernel(x_ref, gpre_ref, wup_ref, cw_ref, cb_ref, wdn_ref, gpost_ref, o_ref,
                     carry_ref, act_ref, *, tm, chunk):
    i = pl.program_id(1)

    @pl.when(i == 0)
    def _():
        carry_ref[...] = jnp.zeros(carry_ref.shape, jnp.float32)

    x = x_ref[...]
    h = _rms_norm(x, gpre_ref[...]).astype(jnp.bfloat16)

    def conv(c0):
        up = _dot(h, wup_ref[:, c0:c0 + chunk])
        ext = jnp.concatenate([carry_ref[:, c0:c0 + chunk], up], axis=0)
        carry_ref[:, c0:c0 + chunk] = up[tm - CONV_HALO:]
        w = cw_ref[:, c0:c0 + chunk]
        y = cb_ref[:, c0:c0 + chunk] + w[CONV_WIDTH - 1:CONV_WIDTH] * up
        for j in range(CONV_WIDTH - 1):
            off = CONV_HALO - (CONV_WIDTH - 1) + j
            y = y + w[j:j + 1] * ext[off:off + tm]
        return y

    for c0 in range(0, D_FF, chunk):
        gate = conv(c0)
        val = conv(D_FF + c0)
        act_ref[:, c0:c0 + chunk] = (jax.nn.gelu(gate, approximate=True) * val).astype(jnp.bfloat16)

    ff = _dot(act_ref[...], wdn_ref[...])
    o_ref[...] = x + _rms_norm(ff, gpost_ref[...])


def _conv_ffn(x, gpre, wup, cw, cb, wdn, gpost):
    b, s, d = x.shape
    tm = FFN_TM
    const2 = lambda shape: pl.BlockSpec(shape, lambda bi, i: (0, 0), pipeline_mode=pl.Buffered(1))
    tile = pl.BlockSpec((None, tm, d), lambda bi, i: (bi, i, 0))
    kern = functools.partial(_conv_ffn_kernel, tm=tm, chunk=FFN_CHUNK)
    return pl.pallas_call(
        kern,
        grid=(b, s // tm),
        in_specs=[
            tile,
            const2((1, d)),
            const2((d, 2 * D_FF)),
            const2((CONV_WIDTH, 2 * D_FF)),
            const2((1, 2 * D_FF)),
            const2((D_FF, d)),
            const2((1, d)),
        ],
        out_specs=tile,
        out_shape=jax.ShapeDtypeStruct((b, s, d), jnp.float32),
        scratch_shapes=[
            pltpu.VMEM((CONV_HALO, 2 * D_FF), jnp.float32),
            pltpu.VMEM((tm, D_FF), jnp.bfloat16),
        ],
        compiler_params=pltpu.CompilerParams(
            dimension_semantics=("arbitrary", "arbitrary"), vmem_limit_bytes=VMEM_LIMIT_BYTES),
        name="conv_ffn",
    )(x, gpre, wup, cw, cb, wdn, gpost)


def kernel(x, mem, norm_mix_pre, w_in, lambda_q1, lambda_k1, lambda_q2, lambda_k2, subln_g, w_attn_branch, pool_w, pool_scale, w_pool_branch, norm_mem, w_mem_kv, w_mem_branch, w_out, norm_mix_post, norm_ffn_pre, w_up, conv_w, conv_b, w_down, norm_ffn_post):
    b, s, d = x.shape
    depth = w_in.shape[0]
    bf = lambda w: w.astype(jnp.bfloat16)
    for l in range(depth):
        lam_init = 0.8 - 0.6 * math.exp(-0.3 * l)
        n_tiles = IN_COLS // PROJ_TN
        w_tiles = bf(w_in[l]).reshape(d, n_tiles, PROJ_TN).transpose(1, 0, 2)
        proj = _in_proj(x.reshape(b * s, d), norm_mix_pre[l][None], w_tiles).reshape(n_tiles, b, s, PROJ_TN)
        km, vm = _mem_kv(mem, norm_mem[l][None], bf(w_mem_kv[l]))
        a = _diff_attn(proj, lambda_q1[l][None], lambda_k1[l][None], lambda_q2[l][None],
                       lambda_k2[l][None], subln_g[l][:, None], lam_init)
        x = _merge(x, a, proj, km, vm, bf(w_attn_branch[l]), bf(pool_w[l]), pool_scale[l][None],
                   bf(w_pool_branch[l]), bf(w_mem_branch[l]), bf(w_out[l]), norm_mix_post[l][None])
        x = _conv_ffn(x, norm_ffn_pre[l][None], bf(w_up[l]), conv_w[l], conv_b[l][None],
                      bf(w_down[l]), norm_ffn_post[l][None])
    return x
```

```python
import functools
import math

import jax
import jax.numpy as jnp
from jax import lax
from jax.experimental import pallas as pl
from jax.experimental.pallas import tpu as pltpu

D_MODEL = 1024
N_MEM_HEADS = 4
DH_MEM = 128
MEM_WIDTH = N_MEM_HEADS * DH_MEM
N_DIFF_HEADS = 8
DH_DIFF = 64
HEAD_W = 2 * DH_DIFF
DIFF_WIDTH = N_DIFF_HEADS * HEAD_W
POOL_WINDOWS = (2, 4, 8, 16)
POOL_GROUP_WIDTH = 128
POOL_WIDTH = len(POOL_WINDOWS) * POOL_GROUP_WIDTH
POOL_HALO = 16
assert all(w == 2 ** (g + 1) for g, w in enumerate(POOL_WINDOWS)) and max(POOL_WINDOWS) <= POOL_HALO
N_BRANCHES = 3
IN_COLS = 3 * DIFF_WIDTH + POOL_WIDTH + MEM_WIDTH + N_BRANCHES * D_MODEL
D_FF = 2816
CONV_WIDTH = 3
CONV_HALO = 8
NORM_EPS = 1e-6
LOG2E = math.log2(math.e)
NEG_BIG = -1e30

VMEM_LIMIT_BYTES = 56 * 1024 * 1024

PROJ_TM = 2048
PROJ_TN = 1024
ATTN_BQ = 2048
ATTN_BK = 256
ATTN_BLOCKS_PER_ITER = 8
ATTN_SCORES_AHEAD = 4
ATTN_CW = 256
BIAS_ROWS = 16
ACC_PAD = 16
MERGE_TM = 512
FFN_TM = 512
FFN_CHUNK = 256


def _rms_norm(x, g):
    return x * lax.rsqrt(jnp.mean(x * x, axis=-1, keepdims=True) + NORM_EPS) * g


def _dot(a, b):
    return jnp.dot(a, b, preferred_element_type=jnp.float32)


def _dot_nt(a, b):
    return lax.dot_general(a, b, (((1,), (1,)), ((), ())), preferred_element_type=jnp.float32)


def _in_proj_kernel(x_ref, g_ref, w_ref, o_ref, h_ref, *, q_scale, n_q_tiles, first_gate_tile):
    j = pl.program_id(1)

    @pl.when(j == 0)
    def _():
        h_ref[...] = _rms_norm(x_ref[...], g_ref[...]).astype(jnp.bfloat16)

    acc = _dot(h_ref[...], w_ref[...])
    scale = jnp.where(j < n_q_tiles, jnp.float32(q_scale),
                      jnp.where(j >= first_gate_tile, jnp.float32(0.5), jnp.float32(1.0)))
    o_ref[...] = (acc * scale).astype(o_ref.dtype)


def _in_proj(x2d, g, w_bf16):
    t = x2d.shape[0]
    grid = (t // PROJ_TM, IN_COLS // PROJ_TN)
    kern = functools.partial(_in_proj_kernel, q_scale=DH_DIFF ** -0.5 * LOG2E,
                             n_q_tiles=DIFF_WIDTH // PROJ_TN,
                             first_gate_tile=(IN_COLS - N_BRANCHES * D_MODEL) // PROJ_TN)
    return pl.pallas_call(
        kern,
        grid=grid,
        in_specs=[
            pl.BlockSpec((PROJ_TM, D_MODEL), lambda i, j: (i, 0)),
            pl.BlockSpec((1, D_MODEL), lambda i, j: (0, 0)),
            pl.BlockSpec((D_MODEL, PROJ_TN), lambda i, j: (0, j)),
        ],
        out_specs=pl.BlockSpec((PROJ_TM, PROJ_TN), lambda i, j: (i, j)),
        out_shape=jax.ShapeDtypeStruct((t, IN_COLS), jnp.bfloat16),
        scratch_shapes=[pltpu.VMEM((PROJ_TM, D_MODEL), jnp.bfloat16)],
        compiler_params=pltpu.CompilerParams(
            dimension_semantics=("arbitrary", "arbitrary"), vmem_limit_bytes=VMEM_LIMIT_BYTES),
        name="in_proj",
    )(x2d, g, w_bf16)


def _mem_kv_kernel(mem_ref, g_ref, w_ref, k_ref, v_ref, *, k_scale):
    mem_n = _rms_norm(mem_ref[...], g_ref[...]).astype(jnp.bfloat16)
    kv = _dot(mem_n, w_ref[...])
    k_ref[...] = (kv[:, :MEM_WIDTH] * k_scale).astype(k_ref.dtype)
    v_ref[...] = kv[:, MEM_WIDTH:].astype(v_ref.dtype)


def _mem_kv(mem, g, w_bf16):
    b, m, _ = mem.shape
    kern = functools.partial(_mem_kv_kernel, k_scale=DH_MEM ** -0.5 * LOG2E)
    out = jax.ShapeDtypeStruct((b, m, MEM_WIDTH), jnp.bfloat16)
    return pl.pallas_call(
        kern,
        grid=(b,),
        in_specs=[
            pl.BlockSpec((None, m, D_MODEL), lambda i: (i, 0, 0)),
            pl.BlockSpec((1, D_MODEL), lambda i: (0, 0)),
            pl.BlockSpec((D_MODEL, 2 * MEM_WIDTH), lambda i: (0, 0)),
        ],
        out_specs=[pl.BlockSpec((None, m, MEM_WIDTH), lambda i: (i, 0, 0))] * 2,
        out_shape=[out, out],
        compiler_params=pltpu.CompilerParams(
            dimension_semantics=("arbitrary",), vmem_limit_bytes=VMEM_LIMIT_BYTES),
        name="mem_kv",
    )(mem, g, w_bf16)


_SKIP = "skip"


def _diff_attn_kernel(slope_ref, lq1_ref, lk1_ref, lq2_ref, lk2_ref, subg_ref,
                      q_ref, k_ref, v_ref, o_ref,
                      qbd_ref, kfeat_ref, m_ref, acc_ref, *, bq, bk, lam_init):
    h = pl.program_id(1)
    i = pl.program_id(2)
    cw = ATTN_CW
    n_sub = bq // cw
    slope = slope_ref[h]

    def split3(x):
        a = x.astype(jnp.bfloat16).astype(jnp.float32)
        b = (x - a).astype(jnp.bfloat16).astype(jnp.float32)
        return a, b, x - a - b

    qt = q_ref[...].T
    feat = lax.broadcasted_iota(jnp.int32, qt.shape, 0)
    zero = jnp.zeros_like(qt)
    qbd_ref[:HEAD_W, :bq] = jnp.where(feat < DH_DIFF, qt, zero)
    qbd_ref[:HEAD_W, bq:] = jnp.where(feat >= DH_DIFF, qt, zero)
    r = lax.broadcasted_iota(jnp.int32, (BIAS_ROWS, bq), 0)
    ii = lax.broadcasted_iota(jnp.int32, (BIAS_ROWS, bq), 1).astype(jnp.float32)
    pieces = split3(jnp.full((BIAS_ROWS, bq), slope, jnp.float32)) + split3(-ii * slope)
    rows = jnp.zeros((BIAS_ROWS, bq), jnp.float32)
    for n, piece in enumerate(pieces):
        rows = jnp.where(r == n, piece, rows)
    rows = rows.astype(jnp.bfloat16)
    qbd_ref[HEAD_W:HEAD_W + BIAS_ROWS, :bq] = rows
    qbd_ref[HEAD_W:HEAD_W + BIAS_ROWS, bq:] = rows
    qbd_ref[HEAD_W + BIAS_ROWS:, :] = jnp.zeros((HEAD_W - BIAS_ROWS, 2 * bq), jnp.bfloat16)

    jj = lax.broadcasted_iota(jnp.int32, (bk, HEAD_W), 0).astype(jnp.float32)
    fcol = lax.broadcasted_iota(jnp.int32, (bk, HEAD_W), 1)
    kfeat_ref[...] = jnp.where(fcol < 3, jj, jnp.where(fcol < 6, 1.0, 0.0)).astype(jnp.bfloat16)

    m_ref[...] = jnp.full(m_ref.shape, NEG_BIG, jnp.float32)
    acc_ref[...] = jnp.zeros(acc_ref.shape, jnp.float32)
    orow = lax.broadcasted_iota(jnp.int32, (ACC_PAD, bk), 0)
    ones_rows = jnp.where(orow == 0, 1.0, 0.0).astype(jnp.bfloat16)

    def run_blocks(blocks):
        steps = []
        for blk, c, masks in blocks:
            keys = pl.ds(pl.multiple_of(blk * bk, bk), bk)
            kb = jnp.concatenate([k_ref[keys, :], kfeat_ref[...]], axis=1)
            vtb = jnp.concatenate([v_ref[keys, :].T, ones_rows], axis=0)
            steps += [(kb, vtb, c, ch, masks[ch % n_sub])
                      for ch in range(2 * n_sub) if masks[ch % n_sub] is not _SKIP]

        def scores(step):
            kb, _, _, ch, mask = step
            s = _dot(kb, qbd_ref[:, ch * cw:(ch + 1) * cw])
            return s if mask is None else jnp.where(mask, s, NEG_BIG)

        def update(step, s):
            _, vtb, c, ch, _ = step
            lanes = slice(ch * cw, (ch + 1) * cw)
            m_prev = m_ref[:, lanes]
            m_next = jnp.maximum(m_prev, jnp.max(s, axis=0, keepdims=True) + c)
            p = jnp.exp2(s - (m_next - c))
            alpha = jnp.exp2(m_prev - m_next)
            m_ref[:, lanes] = m_next
            acc_ref[:, lanes] = alpha * acc_ref[:, lanes] + _dot(vtb, p.astype(jnp.bfloat16))

        ahead = ATTN_SCORES_AHEAD
        pending = [scores(step) for step in steps[:ahead]]
        for n, step in enumerate(steps):
            if n + ahead < len(steps):
                pending.append(scores(steps[n + ahead]))
            update(step, pending.pop(0))

    blocks_per_tile = bq // bk
    unroll = ATTN_BLOCKS_PER_ITER

    def body(jj, carry):
        blocks = []
        for u in range(unroll):
            j = jj * unroll + u
            blocks.append((j, slope * (j * bk - i * bq).astype(jnp.float32), [None] * n_sub))
        run_blocks(blocks)
        return carry

    lax.fori_loop(0, i * (blocks_per_tile // unroll), body, 0)

    kk = lax.broadcasted_iota(jnp.int32, (bk, cw), 0)
    qq = lax.broadcasted_iota(jnp.int32, (bk, cw), 1)
    diag_blocks = []
    for d in range(blocks_per_tile):
        masks = []
        for qs in range(n_sub):
            lo_key, hi_key = d * bk, d * bk + bk - 1
            lo_q, hi_q = qs * cw, qs * cw + cw - 1
            if hi_key <= lo_q:
                masks.append(None)
            elif lo_key > hi_q:
                masks.append(_SKIP)
            else:
                masks.append(kk + lo_key <= qq + lo_q)
        diag_blocks.append((i * blocks_per_tile + d, slope * float(d * bk), masks))
    run_blocks(diag_blocks)

    lam = (jnp.exp(jnp.sum(lq1_ref[...] * lk1_ref[...], axis=1, keepdims=True))
           - jnp.exp(jnp.sum(lq2_ref[...] * lk2_ref[...], axis=1, keepdims=True))
           + lam_init)
    o = acc_ref[:HEAD_W, :] / acc_ref[HEAD_W:HEAD_W + 1, :]
    a = o[:, :bq] - lam * o[:, bq:]
    a = a * lax.rsqrt(jnp.mean(a * a, axis=0, keepdims=True) + NORM_EPS) * subg_ref[...] * (1.0 - lam_init)
    o_ref[...] = a.T.astype(o_ref.dtype)


def _diff_attn(proj, lq1, lk1, lq2, lk2, subg, lam_init):
    b, s, _ = proj.shape
    bq, bk, nh = ATTN_BQ, ATTN_BK, N_DIFF_HEADS
    slopes = jnp.exp2(-8.0 * jnp.arange(1, nh + 1, dtype=jnp.float32) / nh) * LOG2E
    kern = functools.partial(_diff_attn_kernel, bq=bq, bk=bk, lam_init=lam_init)
    small = lambda n: pl.BlockSpec((1, n), lambda bi, h, i: (0, 0))
    return pl.pallas_call(
        kern,
        grid=(b, nh, s // bq),
        in_specs=[
            pl.BlockSpec(memory_space=pltpu.SMEM),
            small(DH_DIFF), small(DH_DIFF), small(DH_DIFF), small(DH_DIFF),
            pl.BlockSpec((HEAD_W, 1), lambda bi, h, i: (0, 0)),
            pl.BlockSpec((None, bq, HEAD_W), lambda bi, h, i: (bi, i, h)),
            pl.BlockSpec((None, s, HEAD_W), lambda bi, h, i: (bi, 0, nh + h)),
            pl.BlockSpec((None, s, HEAD_W), lambda bi, h, i: (bi, 0, 2 * nh + h)),
        ],
        out_specs=pl.BlockSpec((None, bq, HEAD_W), lambda bi, h, i: (bi, i, h)),
        out_shape=jax.ShapeDtypeStruct((b, s, DIFF_WIDTH), jnp.bfloat16),
        scratch_shapes=[
            pltpu.VMEM((2 * HEAD_W, 2 * bq), jnp.bfloat16),
            pltpu.VMEM((bk, HEAD_W), jnp.bfloat16),
            pltpu.VMEM((1, 2 * bq), jnp.float32),
            pltpu.VMEM((HEAD_W + ACC_PAD, 2 * bq), jnp.float32),
        ],
        compiler_params=pltpu.CompilerParams(
            dimension_semantics=("arbitrary", "arbitrary", "arbitrary"),
            vmem_limit_bytes=VMEM_LIMIT_BYTES),
        name="diff_attn",
    )(slopes, lq1, lk1, lq2, lk2, subg, proj, proj, proj)


def _merge_kernel(x_ref, a_ref, u_ref, uh_ref, qm_ref, g0_ref, g1_ref, g2_ref, km_ref, vm_ref,
                  wa_ref, pw_ref, ps_ref, wp_ref, wm_ref, wo_ref, gpost_ref, o_ref, *, tm):
    i = pl.program_id(1)

    y_attn = _dot(a_ref[...], wa_ref[...])

    u = u_ref[...].astype(jnp.float32)
    halo = jnp.where(i > 0, uh_ref[...].astype(jnp.float32), 0.0)
    t = i * tm + lax.broadcasted_iota(jnp.int32, (tm, POOL_GROUP_WIDTH), 0)
    wsum = jnp.concatenate([halo, u], axis=0)
    ys = []
    for g, w in enumerate(POOL_WINDOWS):
        wsum = wsum + pltpu.roll(wsum, w // 2, axis=0)
        ug = u[:, g * POOL_GROUP_WIDTH:(g + 1) * POOL_GROUP_WIDTH]
        count = jnp.minimum(t + 1, w).astype(jnp.float32)
        pooled = wsum[POOL_HALO:, :POOL_GROUP_WIDTH] / count - ug
        wsum = wsum[:, POOL_GROUP_WIDTH:]
        ys.append(_dot(pooled.astype(jnp.bfloat16), pw_ref[g]))
    y = jnp.concatenate(ys, axis=1) * ps_ref[...]
    y_pool = _dot(y.astype(jnp.bfloat16), wp_ref[...])

    outs = []
    for hh in range(N_MEM_HEADS):
        cols = slice(hh * DH_MEM, (hh + 1) * DH_MEM)
        s = _dot_nt(qm_ref[:, cols], km_ref[:, cols])
        p = jnp.exp2(s - jnp.max(s, axis=1, keepdims=True))
        o = _dot(p.astype(jnp.bfloat16), vm_ref[:, cols])
        outs.append(o / jnp.sum(p, axis=1, keepdims=True))
    y_mem = _dot(jnp.concatenate(outs, axis=1).astype(jnp.bfloat16), wm_ref[...])

    mix = y_attn + y_attn * jnp.tanh(g0_ref[...].astype(jnp.float32))
    mix = mix + y_pool + y_pool * jnp.tanh(g1_ref[...].astype(jnp.float32))
    mix = mix + y_mem + y_mem * jnp.tanh(g2_ref[...].astype(jnp.float32))
    z = _dot((0.5 * mix).astype(jnp.bfloat16), wo_ref[...])
    o_ref[...] = x_ref[...] + _rms_norm(z, gpost_ref[...])


def _merge(x, a, proj, km, vm, wa, pw, ps, wp, wm, wo, gpost):
    b, s, d = x.shape
    tm = MERGE_TM
    m = km.shape[1]
    halo_blocks = tm // POOL_HALO
    pool_col = 3 * DIFF_WIDTH // POOL_WIDTH
    qm_col = pool_col + 1
    gate_col = (3 * DIFF_WIDTH + POOL_WIDTH + MEM_WIDTH) // d
    const2 = lambda shape: pl.BlockSpec(shape, lambda bi, i: (0, 0), pipeline_mode=pl.Buffered(1))
    tile = lambda w, c: pl.BlockSpec((None, tm, w), lambda bi, i, c=c: (bi, i, c))
    kern = functools.partial(_merge_kernel, tm=tm)
    return pl.pallas_call(
        kern,
        grid=(b, s // tm),
        in_specs=[
            tile(d, 0),
            tile(DIFF_WIDTH, 0),
            tile(POOL_WIDTH, pool_col),
            pl.BlockSpec((None, POOL_HALO, POOL_WIDTH),
                         lambda bi, i: (bi, jnp.maximum(i * halo_blocks - 1, 0), pool_col)),
            tile(MEM_WIDTH, qm_col),
            tile(d, gate_col), tile(d, gate_col + 1), tile(d, gate_col + 2),
            pl.BlockSpec((None, m, MEM_WIDTH), lambda bi, i: (bi, 0, 0)),
            pl.BlockSpec((None, m, MEM_WIDTH), lambda bi, i: (bi, 0, 0)),
            const2((DIFF_WIDTH, d)),
            pl.BlockSpec(pw.shape, lambda bi, i: (0, 0, 0)),
            const2((1, POOL_WIDTH)),
            const2((POOL_WIDTH, d)),
            const2((MEM_WIDTH, d)),
            const2((d, d)),
            const2((1, d)),
        ],
        out_specs=tile(d, 0),
        out_shape=jax.ShapeDtypeStruct((b, s, d), jnp.float32),
        compiler_params=pltpu.CompilerParams(
            dimension_semantics=("arbitrary", "arbitrary"), vmem_limit_bytes=VMEM_LIMIT_BYTES),
        name="merge",
    )(x, a, proj, proj, proj, proj, proj, proj, km, vm, wa, pw, ps, wp, wm, wo, gpost)


def _conv_ffn_kernel(x_ref, gpre_ref, wup_ref, cw_ref, cb_ref, wdn_ref, gpost_ref, o_ref,
                     carry_ref, act_ref, *, tm, chunk):
    i = pl.program_id(1)

    @pl.when(i == 0)
    def _():
        carry_ref[...] = jnp.zeros(carry_ref.shape, jnp.float32)

    x = x_ref[...]
    h = _rms_norm(x, gpre_ref[...]).astype(jnp.bfloat16)

    def conv(c0):
        up = _dot(h, wup_ref[:, c0:c0 + chunk])
        ext = jnp.concatenate([carry_ref[:, c0:c0 + chunk], up], axis=0)
        carry_ref[:, c0:c0 + chunk] = up[tm - CONV_HALO:]
        w = cw_ref[:, c0:c0 + chunk]
        y = cb_ref[:, c0:c0 + chunk] + w[CONV_WIDTH - 1:CONV_WIDTH] * up
        for j in range(CONV_WIDTH - 1):
            off = CONV_HALO - (CONV_WIDTH - 1) + j
            y = y + w[j:j + 1] * ext[off:off + tm]
        return y

    for c0 in range(0, D_FF, chunk):
        gate = conv(c0)
        val = conv(D_FF + c0)
        act_ref[:, c0:c0 + chunk] = (jax.nn.gelu(gate, approximate=True) * val).astype(jnp.bfloat16)

    ff = _dot(act_ref[...], wdn_ref[...])
    o_ref[...] = x + _rms_norm(ff, gpost_ref[...])


def _conv_ffn(x, gpre, wup, cw, cb, wdn, gpost):
    b, s, d = x.shape
    tm = FFN_TM
    const2 = lambda shape: pl.BlockSpec(shape, lambda bi, i: (0, 0), pipeline_mode=pl.Buffered(1))
    tile = pl.BlockSpec((None, tm, d), lambda bi, i: (bi, i, 0))
    kern = functools.partial(_conv_ffn_kernel, tm=tm, chunk=FFN_CHUNK)
    return pl.pallas_call(
        kern,
        grid=(b, s // tm),
        in_specs=[
            tile,
            const2((1, d)),
            const2((d, 2 * D_FF)),
            const2((CONV_WIDTH, 2 * D_FF)),
            const2((1, 2 * D_FF)),
            const2((D_FF, d)),
            const2((1, d)),
        ],
        out_specs=tile,
        out_shape=jax.ShapeDtypeStruct((b, s, d), jnp.float32),
        scratch_shapes=[
            pltpu.VMEM((CONV_HALO, 2 * D_FF), jnp.float32),
            pltpu.VMEM((tm, D_FF), jnp.bfloat16),
        ],
        compiler_params=pltpu.CompilerParams(
            dimension_semantics=("arbitrary", "arbitrary"), vmem_limit_bytes=VMEM_LIMIT_BYTES),
        name="conv_ffn",
    )(x, gpre, wup, cw, cb, wdn, gpost)


def kernel(x, mem, norm_mix_pre, w_in, lambda_q1, lambda_k1, lambda_q2, lambda_k2, subln_g, w_attn_branch, pool_w, pool_scale, w_pool_branch, norm_mem, w_mem_kv, w_mem_branch, w_out, norm_mix_post, norm_ffn_pre, w_up, conv_w, conv_b, w_down, norm_ffn_post):
    b, s, d = x.shape
    depth = w_in.shape[0]
    bf = lambda w: w.astype(jnp.bfloat16)
    for l in range(depth):
        lam_init = 0.8 - 0.6 * math.exp(-0.3 * l)
        proj = _in_proj(x.reshape(b * s, d), norm_mix_pre[l][None], bf(w_in[l])).reshape(b, s, IN_COLS)
        km, vm = _mem_kv(mem, norm_mem[l][None], bf(w_mem_kv[l]))
        a = _diff_attn(proj, lambda_q1[l][None], lambda_k1[l][None], lambda_q2[l][None],
                       lambda_k2[l][None], subln_g[l][:, None], lam_init)
        x = _merge(x, a, proj, km, vm, bf(w_attn_branch[l]), bf(pool_w[l]), pool_scale[l][None],
                   bf(w_pool_branch[l]), bf(w_mem_branch[l]), bf(w_out[l]), norm_mix_post[l][None])
        x = _conv_ffn(x, norm_ffn_pre[l][None], bf(w_up[l]), conv_w[l], conv_b[l][None],
                      bf(w_down[l]), norm_ffn_post[l][None])
    return x
```

```python
import functools
import math

import jax
import jax.numpy as jnp
from jax import lax
from jax.experimental import pallas as pl
from jax.experimental.pallas import tpu as pltpu

D_MODEL = 1024
N_MEM_HEADS = 4
DH_MEM = 128
MEM_WIDTH = N_MEM_HEADS * DH_MEM
N_DIFF_HEADS = 8
DH_DIFF = 64
HEAD_W = 2 * DH_DIFF
DIFF_WIDTH = N_DIFF_HEADS * HEAD_W
POOL_WINDOWS = (2, 4, 8, 16)
POOL_GROUP_WIDTH = 128
POOL_WIDTH = len(POOL_WINDOWS) * POOL_GROUP_WIDTH
POOL_HALO = 16
assert all(w == 2 ** (g + 1) for g, w in enumerate(POOL_WINDOWS)) and max(POOL_WINDOWS) <= POOL_HALO
N_BRANCHES = 3
IN_COLS = 3 * DIFF_WIDTH + POOL_WIDTH + MEM_WIDTH + N_BRANCHES * D_MODEL
D_FF = 2816
CONV_WIDTH = 3
CONV_HALO = 8
NORM_EPS = 1e-6
LOG2E = math.log2(math.e)
NEG_BIG = -1e30

VMEM_LIMIT_BYTES = 56 * 1024 * 1024

PROJ_TM = 2048
PROJ_TN = 1024
ATTN_BQ = 2048
ATTN_BK = 256
ATTN_BLOCKS_PER_ITER = 8
ATTN_SCORES_AHEAD = 6
ATTN_CW = 256
BIAS_ROWS = 16
ACC_PAD = 16
MERGE_TM = 1024
FFN_TM = 1024
FFN_CHUNK = 256


def _rms_norm(x, g):
    return x * lax.rsqrt(jnp.mean(x * x, axis=-1, keepdims=True) + NORM_EPS) * g


def _dot(a, b):
    return jnp.dot(a, b, preferred_element_type=jnp.float32)


def _dot_nt(a, b):
    return lax.dot_general(a, b, (((1,), (1,)), ((), ())), preferred_element_type=jnp.float32)


def _in_proj_kernel(x_ref, g_ref, w_ref, o_ref, h_ref, *, q_scale, n_q_tiles, first_gate_tile):
    j = pl.program_id(1)

    @pl.when(j == 0)
    def _():
        h_ref[...] = _rms_norm(x_ref[...], g_ref[...]).astype(jnp.bfloat16)

    acc = _dot(h_ref[...], w_ref[...])
    scale = jnp.where(j < n_q_tiles, jnp.float32(q_scale),
                      jnp.where(j >= first_gate_tile, jnp.float32(0.5), jnp.float32(1.0)))
    o_ref[...] = (acc * scale).astype(o_ref.dtype)


def _in_proj(x2d, g, w_bf16):
    t = x2d.shape[0]
    grid = (t // PROJ_TM, IN_COLS // PROJ_TN)
    kern = functools.partial(_in_proj_kernel, q_scale=DH_DIFF ** -0.5 * LOG2E,
                             n_q_tiles=DIFF_WIDTH // PROJ_TN,
                             first_gate_tile=(IN_COLS - N_BRANCHES * D_MODEL) // PROJ_TN)
    return pl.pallas_call(
        kern,
        grid=grid,
        in_specs=[
            pl.BlockSpec((PROJ_TM, D_MODEL), lambda i, j: (i, 0)),
            pl.BlockSpec((1, D_MODEL), lambda i, j: (0, 0)),
            pl.BlockSpec((D_MODEL, PROJ_TN), lambda i, j: (0, j)),
        ],
        out_specs=pl.BlockSpec((PROJ_TM, PROJ_TN), lambda i, j: (i, j)),
        out_shape=jax.ShapeDtypeStruct((t, IN_COLS), jnp.bfloat16),
        scratch_shapes=[pltpu.VMEM((PROJ_TM, D_MODEL), jnp.bfloat16)],
        compiler_params=pltpu.CompilerParams(
            dimension_semantics=("arbitrary", "arbitrary"), vmem_limit_bytes=VMEM_LIMIT_BYTES),
        name="in_proj",
    )(x2d, g, w_bf16)


def _mem_kv_kernel(mem_ref, g_ref, w_ref, k_ref, v_ref, *, k_scale):
    mem_n = _rms_norm(mem_ref[...], g_ref[...]).astype(jnp.bfloat16)
    kv = _dot(mem_n, w_ref[...])
    k_ref[...] = (kv[:, :MEM_WIDTH] * k_scale).astype(k_ref.dtype)
    v_ref[...] = kv[:, MEM_WIDTH:].astype(v_ref.dtype)


def _mem_kv(mem, g, w_bf16):
    b, m, _ = mem.shape
    kern = functools.partial(_mem_kv_kernel, k_scale=DH_MEM ** -0.5 * LOG2E)
    out = jax.ShapeDtypeStruct((b, m, MEM_WIDTH), jnp.bfloat16)
    return pl.pallas_call(
        kern,
        grid=(b,),
        in_specs=[
            pl.BlockSpec((None, m, D_MODEL), lambda i: (i, 0, 0)),
            pl.BlockSpec((1, D_MODEL), lambda i: (0, 0)),
            pl.BlockSpec((D_MODEL, 2 * MEM_WIDTH), lambda i: (0, 0)),
        ],
        out_specs=[pl.BlockSpec((None, m, MEM_WIDTH), lambda i: (i, 0, 0))] * 2,
        out_shape=[out, out],
        compiler_params=pltpu.CompilerParams(
            dimension_semantics=("arbitrary",), vmem_limit_bytes=VMEM_LIMIT_BYTES),
        name="mem_kv",
    )(mem, g, w_bf16)


_SKIP = "skip"


def _diff_attn_kernel(slope_ref, lq1_ref, lk1_ref, lq2_ref, lk2_ref, subg_ref,
                      q_ref, k_ref, v_ref, o_ref,
                      qbd_ref, kfeat_ref, m_ref, acc_ref, s_ref, *, bq, bk, lam_init):
    h = pl.program_id(1)
    i = pl.program_id(2)
    cw = ATTN_CW
    n_sub = bq // cw
    slope = slope_ref[h]

    def split3(x):
        a = x.astype(jnp.bfloat16).astype(jnp.float32)
        b = (x - a).astype(jnp.bfloat16).astype(jnp.float32)
        return a, b, x - a - b

    qt = q_ref[...].T
    feat = lax.broadcasted_iota(jnp.int32, qt.shape, 0)
    zero = jnp.zeros_like(qt)
    qbd_ref[:HEAD_W, :bq] = jnp.where(feat < DH_DIFF, qt, zero)
    qbd_ref[:HEAD_W, bq:] = jnp.where(feat >= DH_DIFF, qt, zero)
    r = lax.broadcasted_iota(jnp.int32, (BIAS_ROWS, bq), 0)
    ii = lax.broadcasted_iota(jnp.int32, (BIAS_ROWS, bq), 1).astype(jnp.float32)
    pieces = split3(jnp.full((BIAS_ROWS, bq), slope, jnp.float32)) + split3(-ii * slope)
    rows = jnp.zeros((BIAS_ROWS, bq), jnp.float32)
    for n, piece in enumerate(pieces):
        rows = jnp.where(r == n, piece, rows)
    rows = rows.astype(jnp.bfloat16)
    qbd_ref[HEAD_W:HEAD_W + BIAS_ROWS, :bq] = rows
    qbd_ref[HEAD_W:HEAD_W + BIAS_ROWS, bq:] = rows
    qbd_ref[HEAD_W + BIAS_ROWS:, :] = jnp.zeros((HEAD_W - BIAS_ROWS, 2 * bq), jnp.bfloat16)

    jj = lax.broadcasted_iota(jnp.int32, (bk, HEAD_W), 0).astype(jnp.float32)
    fcol = lax.broadcasted_iota(jnp.int32, (bk, HEAD_W), 1)
    kfeat_ref[...] = jnp.where(fcol < 3, jj, jnp.where(fcol < 6, 1.0, 0.0)).astype(jnp.bfloat16)

    m_ref[...] = jnp.full(m_ref.shape, NEG_BIG, jnp.float32)
    acc_ref[...] = jnp.zeros(acc_ref.shape, jnp.float32)
    orow = lax.broadcasted_iota(jnp.int32, (ACC_PAD, bk), 0)
    ones_rows = jnp.where(orow == 0, 1.0, 0.0).astype(jnp.bfloat16)

    def run_blocks(blocks):
        steps = []
        for blk, c, masks in blocks:
            keys = pl.ds(pl.multiple_of(blk * bk, bk), bk)
            kb = jnp.concatenate([k_ref[keys, :], kfeat_ref[...]], axis=1)
            vtb = jnp.concatenate([v_ref[keys, :].T, ones_rows], axis=0)
            steps += [(kb, vtb, c, ch, masks[ch % n_sub])
                      for ch in range(2 * n_sub) if masks[ch % n_sub] is not _SKIP]

        def scores(step):
            kb, _, _, ch, mask = step
            s = _dot(kb, qbd_ref[:, ch * cw:(ch + 1) * cw])
            return s if mask is None else jnp.where(mask, s, NEG_BIG)

        def update(step, s):
            _, vtb, c, ch, _ = step
            lanes = slice(ch * cw, (ch + 1) * cw)
            m_prev = m_ref[:, lanes]
            m_next = jnp.maximum(m_prev, jnp.max(s, axis=0, keepdims=True) + c)
            p = jnp.exp2(s - (m_next - c))
            alpha = jnp.exp2(m_prev - m_next)
            m_ref[:, lanes] = m_next
            acc_ref[:, lanes] = alpha * acc_ref[:, lanes] + _dot(vtb, p.astype(jnp.bfloat16))

        ahead = ATTN_SCORES_AHEAD
        slots = s_ref.shape[0]
        for n in range(min(ahead, len(steps))):
            s_ref[n % slots] = scores(steps[n])
        for n, step in enumerate(steps):
            if n + ahead < len(steps):
                s_ref[(n + ahead) % slots] = scores(steps[n + ahead])
            update(step, s_ref[n % slots])

    blocks_per_tile = bq // bk
    unroll = ATTN_BLOCKS_PER_ITER

    def body(jj, carry):
        blocks = []
        for u in range(unroll):
            j = jj * unroll + u
            blocks.append((j, slope * (j * bk - i * bq).astype(jnp.float32), [None] * n_sub))
        run_blocks(blocks)
        return carry

    lax.fori_loop(0, i * (blocks_per_tile // unroll), body, 0)

    kk = lax.broadcasted_iota(jnp.int32, (bk, cw), 0)
    qq = lax.broadcasted_iota(jnp.int32, (bk, cw), 1)
    diag_blocks = []
    for d in range(blocks_per_tile):
        masks = []
        for qs in range(n_sub):
            lo_key, hi_key = d * bk, d * bk + bk - 1
            lo_q, hi_q = qs * cw, qs * cw + cw - 1
            if hi_key <= lo_q:
                masks.append(None)
            elif lo_key > hi_q:
                masks.append(_SKIP)
            else:
                masks.append(kk + lo_key <= qq + lo_q)
        diag_blocks.append((i * blocks_per_tile + d, slope * float(d * bk), masks))
    run_blocks(diag_blocks)

    lam = (jnp.exp(jnp.sum(lq1_ref[...] * lk1_ref[...], axis=1, keepdims=True))
           - jnp.exp(jnp.sum(lq2_ref[...] * lk2_ref[...], axis=1, keepdims=True))
           + lam_init)
    o = acc_ref[:HEAD_W, :] / acc_ref[HEAD_W:HEAD_W + 1, :]
    a = o[:, :bq] - lam * o[:, bq:]
    a = a * lax.rsqrt(jnp.mean(a * a, axis=0, keepdims=True) + NORM_EPS) * subg_ref[...] * (1.0 - lam_init)
    o_ref[...] = a.T.astype(o_ref.dtype)


def _diff_attn(proj, lq1, lk1, lq2, lk2, subg, lam_init):
    b, s, _ = proj.shape
    bq, bk, nh = ATTN_BQ, ATTN_BK, N_DIFF_HEADS
    slopes = jnp.exp2(-8.0 * jnp.arange(1, nh + 1, dtype=jnp.float32) / nh) * LOG2E
    kern = functools.partial(_diff_attn_kernel, bq=bq, bk=bk, lam_init=lam_init)
    small = lambda n: pl.BlockSpec((1, n), lambda bi, h, i: (0, 0))
    return pl.pallas_call(
        kern,
        grid=(b, nh, s // bq),
        in_specs=[
            pl.BlockSpec(memory_space=pltpu.SMEM),
            small(DH_DIFF), small(DH_DIFF), small(DH_DIFF), small(DH_DIFF),
            pl.BlockSpec((HEAD_W, 1), lambda bi, h, i: (0, 0)),
            pl.BlockSpec((None, bq, HEAD_W), lambda bi, h, i: (bi, i, h)),
            pl.BlockSpec((None, s, HEAD_W), lambda bi, h, i: (bi, 0, nh + h)),
            pl.BlockSpec((None, s, HEAD_W), lambda bi, h, i: (bi, 0, 2 * nh + h)),
        ],
        out_specs=pl.BlockSpec((None, bq, HEAD_W), lambda bi, h, i: (bi, i, h)),
        out_shape=jax.ShapeDtypeStruct((b, s, DIFF_WIDTH), jnp.bfloat16),
        scratch_shapes=[
            pltpu.VMEM((2 * HEAD_W, 2 * bq), jnp.bfloat16),
            pltpu.VMEM((bk, HEAD_W), jnp.bfloat16),
            pltpu.VMEM((1, 2 * bq), jnp.float32),
            pltpu.VMEM((HEAD_W + ACC_PAD, 2 * bq), jnp.float32),
            pltpu.VMEM((ATTN_SCORES_AHEAD + 1, bk, ATTN_CW), jnp.float32),
        ],
        compiler_params=pltpu.CompilerParams(
            dimension_semantics=("arbitrary", "arbitrary", "arbitrary"),
            vmem_limit_bytes=VMEM_LIMIT_BYTES),
        name="diff_attn",
    )(slopes, lq1, lk1, lq2, lk2, subg, proj, proj, proj)


def _merge_kernel(x_ref, a_ref, u_ref, uh_ref, qm_ref, g0_ref, g1_ref, g2_ref, km_ref, vm_ref,
                  wa_ref, pw_ref, ps_ref, wp_ref, wm_ref, wo_ref, gpost_ref, o_ref, *, tm):
    i = pl.program_id(1)

    y_attn = _dot(a_ref[...], wa_ref[...])

    u = u_ref[...].astype(jnp.float32)
    halo = jnp.where(i > 0, uh_ref[...].astype(jnp.float32), 0.0)
    t = i * tm + lax.broadcasted_iota(jnp.int32, (tm, POOL_GROUP_WIDTH), 0)
    wsum = jnp.concatenate([halo, u], axis=0)
    ys = []
    for g, w in enumerate(POOL_WINDOWS):
        wsum = wsum + pltpu.roll(wsum, w // 2, axis=0)
        ug = u[:, g * POOL_GROUP_WIDTH:(g + 1) * POOL_GROUP_WIDTH]
        count = jnp.minimum(t + 1, w).astype(jnp.float32)
        pooled = wsum[POOL_HALO:, :POOL_GROUP_WIDTH] / count - ug
        wsum = wsum[:, POOL_GROUP_WIDTH:]
        ys.append(_dot(pooled.astype(jnp.bfloat16), pw_ref[g]))
    y = jnp.concatenate(ys, axis=1) * ps_ref[...]
    y_pool = _dot(y.astype(jnp.bfloat16), wp_ref[...])

    outs = []
    for hh in range(N_MEM_HEADS):
        cols = slice(hh * DH_MEM, (hh + 1) * DH_MEM)
        s = _dot_nt(qm_ref[:, cols], km_ref[:, cols])
        p = jnp.exp2(s - jnp.max(s, axis=1, keepdims=True))
        o = _dot(p.astype(jnp.bfloat16), vm_ref[:, cols])
        outs.append(o / jnp.sum(p, axis=1, keepdims=True))
    y_mem = _dot(jnp.concatenate(outs, axis=1).astype(jnp.bfloat16), wm_ref[...])

    mix = y_attn + y_attn * jnp.tanh(g0_ref[...].astype(jnp.float32))
    mix = mix + y_pool + y_pool * jnp.tanh(g1_ref[...].astype(jnp.float32))
    mix = mix + y_mem + y_mem * jnp.tanh(g2_ref[...].astype(jnp.float32))
    z = _dot((0.5 * mix).astype(jnp.bfloat16), wo_ref[...])
    o_ref[...] = x_ref[...] + _rms_norm(z, gpost_ref[...])


def _merge(x, a, proj, km, vm, wa, pw, ps, wp, wm, wo, gpost):
    b, s, d = x.shape
    tm = MERGE_TM
    m = km.shape[1]
    halo_blocks = tm // POOL_HALO
    pool_col = 3 * DIFF_WIDTH // POOL_WIDTH
    qm_col = pool_col + 1
    gate_col = (3 * DIFF_WIDTH + POOL_WIDTH + MEM_WIDTH) // d
    const2 = lambda shape: pl.BlockSpec(shape, lambda bi, i: (0, 0), pipeline_mode=pl.Buffered(1))
    tile = lambda w, c: pl.BlockSpec((None, tm, w), lambda bi, i, c=c: (bi, i, c))
    kern = functools.partial(_merge_kernel, tm=tm)
    return pl.pallas_call(
        kern,
        grid=(b, s // tm),
        in_specs=[
            tile(d, 0),
            tile(DIFF_WIDTH, 0),
            tile(POOL_WIDTH, pool_col),
            pl.BlockSpec((None, POOL_HALO, POOL_WIDTH),
                         lambda bi, i: (bi, jnp.maximum(i * halo_blocks - 1, 0), pool_col)),
            tile(MEM_WIDTH, qm_col),
            tile(d, gate_col), tile(d, gate_col + 1), tile(d, gate_col + 2),
            pl.BlockSpec((None, m, MEM_WIDTH), lambda bi, i: (bi, 0, 0)),
            pl.BlockSpec((None, m, MEM_WIDTH), lambda bi, i: (bi, 0, 0)),
            const2((DIFF_WIDTH, d)),
            pl.BlockSpec(pw.shape, lambda bi, i: (0, 0, 0)),
            const2((1, POOL_WIDTH)),
            const2((POOL_WIDTH, d)),
            const2((MEM_WIDTH, d)),
            const2((d, d)),
            const2((1, d)),
        ],
        out_specs=tile(d, 0),
        out_shape=jax.ShapeDtypeStruct((b, s, d), jnp.float32),
        compiler_params=pltpu.CompilerParams(
            dimension_semantics=("arbitrary", "arbitrary"), vmem_limit_bytes=VMEM_LIMIT_BYTES),
        name="merge",
    )(x, a, proj, proj, proj, proj, proj, proj, km, vm, wa, pw, ps, wp, wm, wo, gpost)


def _conv_ffn_kernel(x_ref, gpre_ref, wup_ref, cw_ref, cb_ref, wdn_ref, gpost_ref, o_ref,
                     carry_ref, act_ref, *, tm, chunk):
    i = pl.program_id(1)

    @pl.when(i == 0)
    def _():
        carry_ref[...] = jnp.zeros(carry_ref.shape, jnp.float32)

    x = x_ref[...]
    h = _rms_norm(x, gpre_ref[...]).astype(jnp.bfloat16)

    def conv(c0):
        up = _dot(h, wup_ref[:, c0:c0 + chunk])
        ext = jnp.concatenate([carry_ref[:, c0:c0 + chunk], up], axis=0)
        carry_ref[:, c0:c0 + chunk] = up[tm - CONV_HALO:]
        w = cw_ref[:, c0:c0 + chunk]
        y = cb_ref[:, c0:c0 + chunk] + w[CONV_WIDTH - 1:CONV_WIDTH] * up
        for j in range(CONV_WIDTH - 1):
            off = CONV_HALO - (CONV_WIDTH - 1) + j
            y = y + w[j:j + 1] * ext[off:off + tm]
        return y

    for c0 in range(0, D_FF, chunk):
        gate = conv(c0)
        val = conv(D_FF + c0)
        act_ref[:, c0:c0 + chunk] = (jax.nn.gelu(gate, approximate=True) * val).astype(jnp.bfloat16)

    ff = _dot(act_ref[...], wdn_ref[...])
    o_ref[...] = x + _rms_norm(ff, gpost_ref[...])


def _conv_ffn(x, gpre, wup, cw, cb, wdn, gpost):
    b, s, d = x.shape
    tm = FFN_TM
    const2 = lambda shape: pl.BlockSpec(shape, lambda bi, i: (0, 0), pipeline_mode=pl.Buffered(1))
    tile = pl.BlockSpec((None, tm, d), lambda bi, i: (bi, i, 0))
    kern = functools.partial(_conv_ffn_kernel, tm=tm, chunk=FFN_CHUNK)
    return pl.pallas_call(
        kern,
        grid=(b, s // tm),
        in_specs=[
            tile,
            const2((1, d)),
            const2((d, 2 * D_FF)),
            const2((CONV_WIDTH, 2 * D_FF)),
            const2((1, 2 * D_FF)),
            const2((D_FF, d)),
            const2((1, d)),
        ],
        out_specs=tile,
        out_shape=jax.ShapeDtypeStruct((b, s, d), jnp.float32),
        scratch_shapes=[
            pltpu.VMEM((CONV_HALO, 2 * D_FF), jnp.float32),
            pltpu.VMEM((tm, D_FF), jnp.bfloat16),
        ],
        compiler_params=pltpu.CompilerParams(
            dimension_semantics=("arbitrary", "arbitrary"), vmem_limit_bytes=VMEM_LIMIT_BYTES),
        name="conv_ffn",
    )(x, gpre, wup, cw, cb, wdn, gpost)


def kernel(x, mem, norm_mix_pre, w_in, lambda_q1, lambda_k1, lambda_q2, lambda_k2, subln_g, w_attn_branch, pool_w, pool_scale, w_pool_branch, norm_mem, w_mem_kv, w_mem_branch, w_out, norm_mix_post, norm_ffn_pre, w_up, conv_w, conv_b, w_down, norm_ffn_post):
    b, s, d = x.shape
    depth = w_in.shape[0]
    bf = lambda w: w.astype(jnp.bfloat16)
    for l in range(depth):
        lam_init = 0.8 - 0.6 * math.exp(-0.3 * l)
        proj = _in_proj(x.reshape(b * s, d), norm_mix_pre[l][None], bf(w_in[l])).reshape(b, s, IN_COLS)
        km, vm = _mem_kv(mem, norm_mem[l][None], bf(w_mem_kv[l]))
        a = _diff_attn(proj, lambda_q1[l][None], lambda_k1[l][None], lambda_q2[l][None],
                       lambda_k2[l][None], subln_g[l][:, None], lam_init)
        x = _merge(x, a, proj, km, vm, bf(w_attn_branch[l]), bf(pool_w[l]), pool_scale[l][None],
                   bf(w_pool_branch[l]), bf(w_mem_branch[l]), bf(w_out[l]), norm_mix_post[l][None])
        x = _conv_ffn(x, norm_ffn_pre[l][None], bf(w_up[l]), conv_w[l], conv_b[l][None],
                      bf(w_down[l]), norm_ffn_post[l][None])
    return x
```

```python
import functools
import math

import jax
import jax.numpy as jnp
from jax import lax
from jax.experimental import pallas as pl
from jax.experimental.pallas import tpu as pltpu

D_MODEL = 1024
N_MEM_HEADS = 4
DH_MEM = 128
MEM_WIDTH = N_MEM_HEADS * DH_MEM
N_DIFF_HEADS = 8
DH_DIFF = 64
HEAD_W = 2 * DH_DIFF
DIFF_WIDTH = N_DIFF_HEADS * HEAD_W
POOL_WINDOWS = (2, 4, 8, 16)
POOL_GROUP_WIDTH = 128
POOL_WIDTH = len(POOL_WINDOWS) * POOL_GROUP_WIDTH
POOL_HALO = 16
assert all(w == 2 ** (g + 1) for g, w in enumerate(POOL_WINDOWS)) and max(POOL_WINDOWS) <= POOL_HALO
N_BRANCHES = 3
IN_COLS = 3 * DIFF_WIDTH + POOL_WIDTH + MEM_WIDTH + N_BRANCHES * D_MODEL
D_FF = 2816
CONV_WIDTH = 3
CONV_HALO = 8
NORM_EPS = 1e-6
LOG2E = math.log2(math.e)
NEG_BIG = -1e30

VMEM_LIMIT_BYTES = 56 * 1024 * 1024

PROJ_TM = 2048
PROJ_TN = 1024
ATTN_BQ = 2048
ATTN_BK = 256
ATTN_BLOCKS_PER_ITER = 8
ATTN_SCORES_AHEAD = 6
ATTN_SCORE_SLOTS = 8
assert ATTN_SCORES_AHEAD < ATTN_SCORE_SLOTS
ATTN_CW = 256
BIAS_ROWS = 16
ACC_PAD = 16
MERGE_TM = 1024
FFN_TM = 1024
FFN_CHUNK = 256


def _rms_norm(x, g):
    return x * lax.rsqrt(jnp.mean(x * x, axis=-1, keepdims=True) + NORM_EPS) * g


def _dot(a, b):
    return jnp.dot(a, b, preferred_element_type=jnp.float32)


def _dot_nt(a, b):
    return lax.dot_general(a, b, (((1,), (1,)), ((), ())), preferred_element_type=jnp.float32)


def _in_proj_kernel(x_ref, g_ref, w_ref, o_ref, h_ref, *, q_scale, n_q_tiles, first_gate_tile):
    j = pl.program_id(1)

    @pl.when(j == 0)
    def _():
        h_ref[...] = _rms_norm(x_ref[...], g_ref[...]).astype(jnp.bfloat16)

    acc = _dot(h_ref[...], w_ref[...].astype(jnp.bfloat16))
    scale = jnp.where(j < n_q_tiles, jnp.float32(q_scale),
                      jnp.where(j >= first_gate_tile, jnp.float32(0.5), jnp.float32(1.0)))
    o_ref[...] = (acc * scale).astype(o_ref.dtype)


def _in_proj(x2d, g, w):
    t = x2d.shape[0]
    grid = (t // PROJ_TM, IN_COLS // PROJ_TN)
    kern = functools.partial(_in_proj_kernel, q_scale=DH_DIFF ** -0.5 * LOG2E,
                             n_q_tiles=DIFF_WIDTH // PROJ_TN,
                             first_gate_tile=(IN_COLS - N_BRANCHES * D_MODEL) // PROJ_TN)
    return pl.pallas_call(
        kern,
        grid=grid,
        in_specs=[
            pl.BlockSpec((PROJ_TM, D_MODEL), lambda i, j: (i, 0)),
            pl.BlockSpec((1, D_MODEL), lambda i, j: (0, 0)),
            pl.BlockSpec((D_MODEL, PROJ_TN), lambda i, j: (0, j)),
        ],
        out_specs=pl.BlockSpec((PROJ_TM, PROJ_TN), lambda i, j: (i, j)),
        out_shape=jax.ShapeDtypeStruct((t, IN_COLS), jnp.bfloat16),
        scratch_shapes=[pltpu.VMEM((PROJ_TM, D_MODEL), jnp.bfloat16)],
        compiler_params=pltpu.CompilerParams(
            dimension_semantics=("arbitrary", "arbitrary"), vmem_limit_bytes=VMEM_LIMIT_BYTES),
        name="in_proj",
    )(x2d, g, w)


def _mem_kv_kernel(mem_ref, g_ref, w_ref, k_ref, v_ref, *, k_scale):
    mem_n = _rms_norm(mem_ref[...], g_ref[...]).astype(jnp.bfloat16)
    kv = _dot(mem_n, w_ref[...])
    k_ref[...] = (kv[:, :MEM_WIDTH] * k_scale).astype(k_ref.dtype)
    v_ref[...] = kv[:, MEM_WIDTH:].astype(v_ref.dtype)


def _mem_kv(mem, g, w_bf16):
    b, m, _ = mem.shape
    kern = functools.partial(_mem_kv_kernel, k_scale=DH_MEM ** -0.5 * LOG2E)
    out = jax.ShapeDtypeStruct((b, m, MEM_WIDTH), jnp.bfloat16)
    return pl.pallas_call(
        kern,
        grid=(b,),
        in_specs=[
            pl.BlockSpec((None, m, D_MODEL), lambda i: (i, 0, 0)),
            pl.BlockSpec((1, D_MODEL), lambda i: (0, 0)),
            pl.BlockSpec((D_MODEL, 2 * MEM_WIDTH), lambda i: (0, 0)),
        ],
        out_specs=[pl.BlockSpec((None, m, MEM_WIDTH), lambda i: (i, 0, 0))] * 2,
        out_shape=[out, out],
        compiler_params=pltpu.CompilerParams(
            dimension_semantics=("arbitrary",), vmem_limit_bytes=VMEM_LIMIT_BYTES),
        name="mem_kv",
    )(mem, g, w_bf16)


_SKIP = "skip"


def _diff_attn_kernel(slope_ref, lq1_ref, lk1_ref, lq2_ref, lk2_ref, subg_ref,
                      q_ref, k_ref, v_ref, o_ref,
                      qbd_ref, kfeat_ref, m_ref, acc_ref, s_ref, *, bq, bk, lam_init):
    h = pl.program_id(1)
    i = pl.program_id(2)
    cw = ATTN_CW
    n_sub = bq // cw
    slope = slope_ref[h]

    def split3(x):
        a = x.astype(jnp.bfloat16).astype(jnp.float32)
        b = (x - a).astype(jnp.bfloat16).astype(jnp.float32)
        return a, b, x - a - b

    qt = q_ref[...].T
    feat = lax.broadcasted_iota(jnp.int32, qt.shape, 0)
    zero = jnp.zeros_like(qt)
    qbd_ref[:HEAD_W, :bq] = jnp.where(feat < DH_DIFF, qt, zero)
    qbd_ref[:HEAD_W, bq:] = jnp.where(feat >= DH_DIFF, qt, zero)
    r = lax.broadcasted_iota(jnp.int32, (BIAS_ROWS, bq), 0)
    ii = lax.broadcasted_iota(jnp.int32, (BIAS_ROWS, bq), 1).astype(jnp.float32)
    pieces = split3(jnp.full((BIAS_ROWS, bq), slope, jnp.float32)) + split3(-ii * slope)
    rows = jnp.zeros((BIAS_ROWS, bq), jnp.float32)
    for n, piece in enumerate(pieces):
        rows = jnp.where(r == n, piece, rows)
    rows = rows.astype(jnp.bfloat16)
    qbd_ref[HEAD_W:HEAD_W + BIAS_ROWS, :bq] = rows
    qbd_ref[HEAD_W:HEAD_W + BIAS_ROWS, bq:] = rows
    qbd_ref[HEAD_W + BIAS_ROWS:, :] = jnp.zeros((HEAD_W - BIAS_ROWS, 2 * bq), jnp.bfloat16)

    jj = lax.broadcasted_iota(jnp.int32, (bk, HEAD_W), 0).astype(jnp.float32)
    fcol = lax.broadcasted_iota(jnp.int32, (bk, HEAD_W), 1)
    kfeat_ref[...] = jnp.where(fcol < 3, jj, jnp.where(fcol < 6, 1.0, 0.0)).astype(jnp.bfloat16)

    m_ref[...] = jnp.full(m_ref.shape, NEG_BIG, jnp.float32)
    acc_ref[...] = jnp.zeros(acc_ref.shape, jnp.float32)
    orow = lax.broadcasted_iota(jnp.int32, (ACC_PAD, bk), 0)
    ones_rows = jnp.where(orow == 0, 1.0, 0.0).astype(jnp.bfloat16)

    ahead = ATTN_SCORES_AHEAD
    slots = s_ref.shape[0]

    def key_block(blk):
        keys = pl.ds(pl.multiple_of(blk * bk, bk), bk)
        return jnp.concatenate([k_ref[keys, :], kfeat_ref[...]], axis=1), keys

    def scores(kb, ch):
        return _dot(kb, qbd_ref[:, ch * cw:(ch + 1) * cw])

    def run_blocks(blocks, next_blk):
        steps = []
        for blk, c, masks in blocks:
            kb, keys = key_block(blk)
            vtb = jnp.concatenate([v_ref[keys, :].T, ones_rows], axis=0)
            steps += [(kb, vtb, c, ch, masks[ch % n_sub])
                      for ch in range(2 * n_sub) if masks[ch % n_sub] is not _SKIP]
        assert [st[3] for st in steps[:ahead]] == list(range(ahead))
        if next_blk is not None:
            assert len(steps) % slots == 0
            kb_next, _ = key_block(next_blk)

        def update(step, s):
            _, vtb, c, ch, mask = step
            if mask is not None:
                s = jnp.where(mask, s, NEG_BIG)
            lanes = slice(ch * cw, (ch + 1) * cw)
            m_prev = m_ref[:, lanes]
            m_next = jnp.maximum(m_prev, jnp.max(s, axis=0, keepdims=True) + c)
            p = jnp.exp2(s - (m_next - c))
            alpha = jnp.exp2(m_prev - m_next)
            m_ref[:, lanes] = m_next
            acc_ref[:, lanes] = alpha * acc_ref[:, lanes] + _dot(vtb, p.astype(jnp.bfloat16))

        for n, step in enumerate(steps):
            if n + ahead < len(steps):
                s_ref[(n + ahead) % slots] = scores(steps[n + ahead][0], steps[n + ahead][3])
            elif next_blk is not None:
                s_ref[(n + ahead) % slots] = scores(kb_next, n + ahead - len(steps))
            update(step, s_ref[n % slots])

    blocks_per_tile = bq // bk
    unroll = ATTN_BLOCKS_PER_ITER

    kb0, _ = key_block(0)
    for ch in range(ahead):
        s_ref[ch] = scores(kb0, ch)

    def body(jj, carry):
        blocks = []
        for u in range(unroll):
            j = jj * unroll + u
            blocks.append((j, slope * (j * bk - i * bq).astype(jnp.float32), [None] * n_sub))
        run_blocks(blocks, (jj + 1) * unroll)
        return carry

    lax.fori_loop(0, i * (blocks_per_tile // unroll), body, 0)

    kk = lax.broadcasted_iota(jnp.int32, (bk, cw), 0)
    qq = lax.broadcasted_iota(jnp.int32, (bk, cw), 1)
    diag_blocks = []
    for d in range(blocks_per_tile):
        masks = []
        for qs in range(n_sub):
            lo_key, hi_key = d * bk, d * bk + bk - 1
            lo_q, hi_q = qs * cw, qs * cw + cw - 1
            if hi_key <= lo_q:
                masks.append(None)
            elif lo_key > hi_q:
                masks.append(_SKIP)
            else:
                masks.append(kk + lo_key <= qq + lo_q)
        diag_blocks.append((i * blocks_per_tile + d, slope * float(d * bk), masks))
    run_blocks(diag_blocks, None)

    lam = (jnp.exp(jnp.sum(lq1_ref[...] * lk1_ref[...], axis=1, keepdims=True))
           - jnp.exp(jnp.sum(lq2_ref[...] * lk2_ref[...], axis=1, keepdims=True))
           + lam_init)
    o = acc_ref[:HEAD_W, :] / acc_ref[HEAD_W:HEAD_W + 1, :]
    a = o[:, :bq] - lam * o[:, bq:]
    a = a * lax.rsqrt(jnp.mean(a * a, axis=0, keepdims=True) + NORM_EPS) * subg_ref[...] * (1.0 - lam_init)
    o_ref[...] = a.T.astype(o_ref.dtype)


def _diff_attn(proj, lq1, lk1, lq2, lk2, subg, lam_init):
    b, s, _ = proj.shape
    bq, bk, nh = ATTN_BQ, ATTN_BK, N_DIFF_HEADS
    slopes = jnp.exp2(-8.0 * jnp.arange(1, nh + 1, dtype=jnp.float32) / nh) * LOG2E
    kern = functools.partial(_diff_attn_kernel, bq=bq, bk=bk, lam_init=lam_init)
    small = lambda n: pl.BlockSpec((1, n), lambda bi, h, i: (0, 0))
    return pl.pallas_call(
        kern,
        grid=(b, nh, s // bq),
        in_specs=[
            pl.BlockSpec(memory_space=pltpu.SMEM),
            small(DH_DIFF), small(DH_DIFF), small(DH_DIFF), small(DH_DIFF),
            pl.BlockSpec((HEAD_W, 1), lambda bi, h, i: (0, 0)),
            pl.BlockSpec((None, bq, HEAD_W), lambda bi, h, i: (bi, i, h)),
            pl.BlockSpec((None, s, HEAD_W), lambda bi, h, i: (bi, 0, nh + h)),
            pl.BlockSpec((None, s, HEAD_W), lambda bi, h, i: (bi, 0, 2 * nh + h)),
        ],
        out_specs=pl.BlockSpec((None, bq, HEAD_W), lambda bi, h, i: (bi, i, h)),
        out_shape=jax.ShapeDtypeStruct((b, s, DIFF_WIDTH), jnp.bfloat16),
        scratch_shapes=[
            pltpu.VMEM((2 * HEAD_W, 2 * bq), jnp.bfloat16),
            pltpu.VMEM((bk, HEAD_W), jnp.bfloat16),
            pltpu.VMEM((1, 2 * bq), jnp.float32),
            pltpu.VMEM((HEAD_W + ACC_PAD, 2 * bq), jnp.float32),
            pltpu.VMEM((ATTN_SCORE_SLOTS, bk, ATTN_CW), jnp.float32),
        ],
        compiler_params=pltpu.CompilerParams(
            dimension_semantics=("arbitrary", "arbitrary", "arbitrary"),
            vmem_limit_bytes=VMEM_LIMIT_BYTES),
        name="diff_attn",
    )(slopes, lq1, lk1, lq2, lk2, subg, proj, proj, proj)


def _merge_kernel(x_ref, a_ref, u_ref, uh_ref, qm_ref, g0_ref, g1_ref, g2_ref, km_ref, vm_ref,
                  wa_ref, pw_ref, ps_ref, wp_ref, wm_ref, wo_ref, gpost_ref, o_ref, *, tm):
    i = pl.program_id(1)

    y_attn = _dot(a_ref[...], wa_ref[...])

    u = u_ref[...].astype(jnp.float32)
    halo = jnp.where(i > 0, uh_ref[...].astype(jnp.float32), 0.0)
    t = i * tm + lax.broadcasted_iota(jnp.int32, (tm, POOL_GROUP_WIDTH), 0)
    wsum = jnp.concatenate([halo, u], axis=0)
    ys = []
    for g, w in enumerate(POOL_WINDOWS):
        wsum = wsum + pltpu.roll(wsum, w // 2, axis=0)
        ug = u[:, g * POOL_GROUP_WIDTH:(g + 1) * POOL_GROUP_WIDTH]
        count = jnp.minimum(t + 1, w).astype(jnp.float32)
        pooled = wsum[POOL_HALO:, :POOL_GROUP_WIDTH] / count - ug
        wsum = wsum[:, POOL_GROUP_WIDTH:]
        ys.append(_dot(pooled.astype(jnp.bfloat16), pw_ref[g]))
    y = jnp.concatenate(ys, axis=1) * ps_ref[...]
    y_pool = _dot(y.astype(jnp.bfloat16), wp_ref[...])

    outs = []
    for hh in range(N_MEM_HEADS):
        cols = slice(hh * DH_MEM, (hh + 1) * DH_MEM)
        s = _dot_nt(qm_ref[:, cols], km_ref[:, cols])
        p = jnp.exp2(s - jnp.max(s, axis=1, keepdims=True))
        o = _dot(p.astype(jnp.bfloat16), vm_ref[:, cols])
        outs.append(o / jnp.sum(p, axis=1, keepdims=True))
    y_mem = _dot(jnp.concatenate(outs, axis=1).astype(jnp.bfloat16), wm_ref[...])

    mix = y_attn + y_attn * jnp.tanh(g0_ref[...].astype(jnp.float32))
    mix = mix + y_pool + y_pool * jnp.tanh(g1_ref[...].astype(jnp.float32))
    mix = mix + y_mem + y_mem * jnp.tanh(g2_ref[...].astype(jnp.float32))
    z = _dot((0.5 * mix).astype(jnp.bfloat16), wo_ref[...])
    o_ref[...] = x_ref[...] + _rms_norm(z, gpost_ref[...])


def _merge(x, a, proj, km, vm, wa, pw, ps, wp, wm, wo, gpost):
    b, s, d = x.shape
    tm = MERGE_TM
    m = km.shape[1]
    halo_blocks = tm // POOL_HALO
    pool_col = 3 * DIFF_WIDTH // POOL_WIDTH
    qm_col = pool_col + 1
    gate_col = (3 * DIFF_WIDTH + POOL_WIDTH + MEM_WIDTH) // d
    const2 = lambda shape: pl.BlockSpec(shape, lambda bi, i: (0, 0), pipeline_mode=pl.Buffered(1))
    tile = lambda w, c: pl.BlockSpec((None, tm, w), lambda bi, i, c=c: (bi, i, c))
    kern = functools.partial(_merge_kernel, tm=tm)
    return pl.pallas_call(
        kern,
        grid=(b, s // tm),
        in_specs=[
            tile(d, 0),
            tile(DIFF_WIDTH, 0),
            tile(POOL_WIDTH, pool_col),
            pl.BlockSpec((None, POOL_HALO, POOL_WIDTH),
                         lambda bi, i: (bi, jnp.maximum(i * halo_blocks - 1, 0), pool_col)),
            tile(MEM_WIDTH, qm_col),
            tile(d, gate_col), tile(d, gate_col + 1), tile(d, gate_col + 2),
            pl.BlockSpec((None, m, MEM_WIDTH), lambda bi, i: (bi, 0, 0)),
            pl.BlockSpec((None, m, MEM_WIDTH), lambda bi, i: (bi, 0, 0)),
            const2((DIFF_WIDTH, d)),
            pl.BlockSpec(pw.shape, lambda bi, i: (0, 0, 0)),
            const2((1, POOL_WIDTH)),
            const2((POOL_WIDTH, d)),
            const2((MEM_WIDTH, d)),
            const2((d, d)),
            const2((1, d)),
        ],
        out_specs=tile(d, 0),
        out_shape=jax.ShapeDtypeStruct((b, s, d), jnp.float32),
        compiler_params=pltpu.CompilerParams(
            dimension_semantics=("arbitrary", "arbitrary"), vmem_limit_bytes=VMEM_LIMIT_BYTES),
        name="merge",
    )(x, a, proj, proj, proj, proj, proj, proj, km, vm, wa, pw, ps, wp, wm, wo, gpost)


def _conv_ffn_kernel(x_ref, gpre_ref, wup_ref, cw_ref, cb_ref, wdn_ref, gpost_ref, o_ref,
                     carry_ref, act_ref, *, tm, chunk):
    i = pl.program_id(1)

    @pl.when(i == 0)
    def _():
        carry_ref[...] = jnp.zeros(carry_ref.shape, jnp.float32)

    x = x_ref[...]
    h = _rms_norm(x, gpre_ref[...]).astype(jnp.bfloat16)

    def conv(c0):
        up = _dot(h, wup_ref[:, c0:c0 + chunk])
        ext = jnp.concatenate([carry_ref[:, c0:c0 + chunk], up], axis=0)
        carry_ref[:, c0:c0 + chunk] = up[tm - CONV_HALO:]
        w = cw_ref[:, c0:c0 + chunk]
        y = cb_ref[:, c0:c0 + chunk] + w[CONV_WIDTH - 1:CONV_WIDTH] * up
        for j in range(CONV_WIDTH - 1):
            off = CONV_HALO - (CONV_WIDTH - 1) + j
            y = y + w[j:j + 1] * ext[off:off + tm]
        return y

    for c0 in range(0, D_FF, chunk):
        gate = conv(c0)
        val = conv(D_FF + c0)
        act_ref[:, c0:c0 + chunk] = (jax.nn.gelu(gate, approximate=True) * val).astype(jnp.bfloat16)

    ff = _dot(act_ref[...], wdn_ref[...])
    o_ref[...] = x + _rms_norm(ff, gpost_ref[...])


def _conv_ffn(x, gpre, wup, cw, cb, wdn, gpost):
    b, s, d = x.shape
    tm = FFN_TM
    const2 = lambda shape: pl.BlockSpec(shape, lambda bi, i: (0, 0), pipeline_mode=pl.Buffered(1))
    tile = pl.BlockSpec((None, tm, d), lambda bi, i: (bi, i, 0))
    kern = functools.partial(_conv_ffn_kernel, tm=tm, chunk=FFN_CHUNK)
    return pl.pallas_call(
        kern,
        grid=(b, s // tm),
        in_specs=[
            tile,
            const2((1, d)),
            const2((d, 2 * D_FF)),
            const2((CONV_WIDTH, 2 * D_FF)),
            const2((1, 2 * D_FF)),
            const2((D_FF, d)),
            const2((1, d)),
        ],
        out_specs=tile,
        out_shape=jax.ShapeDtypeStruct((b, s, d), jnp.float32),
        scratch_shapes=[
            pltpu.VMEM((CONV_HALO, 2 * D_FF), jnp.float32),
            pltpu.VMEM((tm, D_FF), jnp.bfloat16),
        ],
        compiler_params=pltpu.CompilerParams(
            dimension_semantics=("arbitrary", "arbitrary"), vmem_limit_bytes=VMEM_LIMIT_BYTES),
        name="conv_ffn",
    )(x, gpre, wup, cw, cb, wdn, gpost)


def kernel(x, mem, norm_mix_pre, w_in, lambda_q1, lambda_k1, lambda_q2, lambda_k2, subln_g, w_attn_branch, pool_w, pool_scale, w_pool_branch, norm_mem, w_mem_kv, w_mem_branch, w_out, norm_mix_post, norm_ffn_pre, w_up, conv_w, conv_b, w_down, norm_ffn_post):
    b, s, d = x.shape
    depth = w_in.shape[0]
    bf = lambda w: w.astype(jnp.bfloat16)
    for l in range(depth):
        lam_init = 0.8 - 0.6 * math.exp(-0.3 * l)
        proj = _in_proj(x.reshape(b * s, d), norm_mix_pre[l][None], w_in[l]).reshape(b, s, IN_COLS)
        km, vm = _mem_kv(mem, norm_mem[l][None], bf(w_mem_kv[l]))
        a = _diff_attn(proj, lambda_q1[l][None], lambda_k1[l][None], lambda_q2[l][None],
                       lambda_k2[l][None], subln_g[l][:, None], lam_init)
        x = _merge(x, a, proj, km, vm, bf(w_attn_branch[l]), bf(pool_w[l]), pool_scale[l][None],
                   bf(w_pool_branch[l]), bf(w_mem_branch[l]), bf(w_out[l]), norm_mix_post[l][None])
        x = _conv_ffn(x, norm_ffn_pre[l][None], bf(w_up[l]), conv_w[l], conv_b[l][None],
                      bf(w_down[l]), norm_ffn_post[l][None])
    return x
```

```python
import functools
import math

import jax
import jax.numpy as jnp
from jax import lax
from jax.experimental import pallas as pl
from jax.experimental.pallas import tpu as pltpu

D_MODEL = 1024
N_MEM_HEADS = 4
DH_MEM = 128
MEM_WIDTH = N_MEM_HEADS * DH_MEM
N_DIFF_HEADS = 8
DH_DIFF = 64
HEAD_W = 2 * DH_DIFF
DIFF_WIDTH = N_DIFF_HEADS * HEAD_W
POOL_WINDOWS = (2, 4, 8, 16)
POOL_GROUP_WIDTH = 128
POOL_WIDTH = len(POOL_WINDOWS) * POOL_GROUP_WIDTH
POOL_HALO = 16
assert all(w == 2 ** (g + 1) for g, w in enumerate(POOL_WINDOWS)) and max(POOL_WINDOWS) <= POOL_HALO
N_BRANCHES = 3
IN_COLS = 3 * DIFF_WIDTH + POOL_WIDTH + MEM_WIDTH + N_BRANCHES * D_MODEL
D_FF = 2816
CONV_WIDTH = 3
CONV_HALO = 8
NORM_EPS = 1e-6
LOG2E = math.log2(math.e)
NEG_BIG = -1e30

VMEM_LIMIT_BYTES = 56 * 1024 * 1024

PROJ_TM = 2048
PROJ_TN = 1024
ATTN_BQ = 4096
ATTN_BK = 256
ATTN_BLOCKS_PER_ITER = 8
ATTN_SCORES_AHEAD = 6
ATTN_CW = 256
BIAS_ROWS = 16
ACC_PAD = 16
MERGE_TM = 1024
FFN_TM = 1024
FFN_CHUNK = 256


def _rms_norm(x, g):
    return x * lax.rsqrt(jnp.mean(x * x, axis=-1, keepdims=True) + NORM_EPS) * g


def _dot(a, b):
    return jnp.dot(a, b, preferred_element_type=jnp.float32)


def _dot_nt(a, b):
    return lax.dot_general(a, b, (((1,), (1,)), ((), ())), preferred_element_type=jnp.float32)


def _in_proj_kernel(x_ref, g_ref, w_ref, o_ref, h_ref, *, q_scale, n_q_tiles, first_gate_tile):
    j = pl.program_id(1)

    @pl.when(j == 0)
    def _():
        h_ref[...] = _rms_norm(x_ref[...], g_ref[...]).astype(jnp.bfloat16)

    acc = _dot(h_ref[...], w_ref[...].astype(jnp.bfloat16))
    scale = jnp.where(j < n_q_tiles, jnp.float32(q_scale),
                      jnp.where(j >= first_gate_tile, jnp.float32(0.5), jnp.float32(1.0)))
    o_ref[...] = (acc * scale).astype(o_ref.dtype)


def _in_proj(x2d, g, w):
    t = x2d.shape[0]
    grid = (t // PROJ_TM, IN_COLS // PROJ_TN)
    kern = functools.partial(_in_proj_kernel, q_scale=DH_DIFF ** -0.5 * LOG2E,
                             n_q_tiles=DIFF_WIDTH // PROJ_TN,
                             first_gate_tile=(IN_COLS - N_BRANCHES * D_MODEL) // PROJ_TN)
    return pl.pallas_call(
        kern,
        grid=grid,
        in_specs=[
            pl.BlockSpec((PROJ_TM, D_MODEL), lambda i, j: (i, 0)),
            pl.BlockSpec((1, D_MODEL), lambda i, j: (0, 0)),
            pl.BlockSpec((D_MODEL, PROJ_TN), lambda i, j: (0, j)),
        ],
        out_specs=pl.BlockSpec((PROJ_TM, PROJ_TN), lambda i, j: (i, j)),
        out_shape=jax.ShapeDtypeStruct((t, IN_COLS), jnp.bfloat16),
        scratch_shapes=[pltpu.VMEM((PROJ_TM, D_MODEL), jnp.bfloat16)],
        compiler_params=pltpu.CompilerParams(
            dimension_semantics=("arbitrary", "arbitrary"), vmem_limit_bytes=VMEM_LIMIT_BYTES),
        name="in_proj",
    )(x2d, g, w)


def _mem_kv_kernel(mem_ref, g_ref, w_ref, k_ref, v_ref, *, k_scale):
    mem_n = _rms_norm(mem_ref[...], g_ref[...]).astype(jnp.bfloat16)
    kv = _dot(mem_n, w_ref[...])
    k_ref[...] = (kv[:, :MEM_WIDTH] * k_scale).astype(k_ref.dtype)
    v_ref[...] = kv[:, MEM_WIDTH:].astype(v_ref.dtype)


def _mem_kv(mem, g, w_bf16):
    b, m, _ = mem.shape
    kern = functools.partial(_mem_kv_kernel, k_scale=DH_MEM ** -0.5 * LOG2E)
    out = jax.ShapeDtypeStruct((b, m, MEM_WIDTH), jnp.bfloat16)
    return pl.pallas_call(
        kern,
        grid=(b,),
        in_specs=[
            pl.BlockSpec((None, m, D_MODEL), lambda i: (i, 0, 0)),
            pl.BlockSpec((1, D_MODEL), lambda i: (0, 0)),
            pl.BlockSpec((D_MODEL, 2 * MEM_WIDTH), lambda i: (0, 0)),
        ],
        out_specs=[pl.BlockSpec((None, m, MEM_WIDTH), lambda i: (i, 0, 0))] * 2,
        out_shape=[out, out],
        compiler_params=pltpu.CompilerParams(
            dimension_semantics=("arbitrary",), vmem_limit_bytes=VMEM_LIMIT_BYTES),
        name="mem_kv",
    )(mem, g, w_bf16)


_SKIP = "skip"


def _diff_attn_kernel(slope_ref, lq1_ref, lk1_ref, lq2_ref, lk2_ref, subg_ref,
                      q_ref, k_ref, v_ref, o_ref,
                      qbd_ref, kfeat_ref, m_ref, acc_ref, s_ref, *, bq, bk, lam_init):
    h = pl.program_id(1)
    i = pl.program_id(2)
    cw = ATTN_CW
    n_sub = bq // cw
    slope = slope_ref[h]

    def split3(x):
        a = x.astype(jnp.bfloat16).astype(jnp.float32)
        b = (x - a).astype(jnp.bfloat16).astype(jnp.float32)
        return a, b, x - a - b

    qt = q_ref[...].T
    feat = lax.broadcasted_iota(jnp.int32, qt.shape, 0)
    zero = jnp.zeros_like(qt)
    qbd_ref[:HEAD_W, :bq] = jnp.where(feat < DH_DIFF, qt, zero)
    qbd_ref[:HEAD_W, bq:] = jnp.where(feat >= DH_DIFF, qt, zero)
    r = lax.broadcasted_iota(jnp.int32, (BIAS_ROWS, bq), 0)
    ii = lax.broadcasted_iota(jnp.int32, (BIAS_ROWS, bq), 1).astype(jnp.float32)
    pieces = split3(jnp.full((BIAS_ROWS, bq), slope, jnp.float32)) + split3(-ii * slope)
    rows = jnp.zeros((BIAS_ROWS, bq), jnp.float32)
    for n, piece in enumerate(pieces):
        rows = jnp.where(r == n, piece, rows)
    rows = rows.astype(jnp.bfloat16)
    qbd_ref[HEAD_W:HEAD_W + BIAS_ROWS, :bq] = rows
    qbd_ref[HEAD_W:HEAD_W + BIAS_ROWS, bq:] = rows
    qbd_ref[HEAD_W + BIAS_ROWS:, :] = jnp.zeros((HEAD_W - BIAS_ROWS, 2 * bq), jnp.bfloat16)

    jj = lax.broadcasted_iota(jnp.int32, (bk, HEAD_W), 0).astype(jnp.float32)
    fcol = lax.broadcasted_iota(jnp.int32, (bk, HEAD_W), 1)
    kfeat_ref[...] = jnp.where(fcol < 3, jj, jnp.where(fcol < 6, 1.0, 0.0)).astype(jnp.bfloat16)

    m_ref[...] = jnp.full(m_ref.shape, NEG_BIG, jnp.float32)
    acc_ref[...] = jnp.zeros(acc_ref.shape, jnp.float32)
    orow = lax.broadcasted_iota(jnp.int32, (ACC_PAD, bk), 0)
    ones_rows = jnp.where(orow == 0, 1.0, 0.0).astype(jnp.bfloat16)

    def run_blocks(blocks):
        steps = []
        for blk, c, masks in blocks:
            keys = pl.ds(pl.multiple_of(blk * bk, bk), bk)
            kb = jnp.concatenate([k_ref[keys, :], kfeat_ref[...]], axis=1)
            vtb = jnp.concatenate([v_ref[keys, :].T, ones_rows], axis=0)
            steps += [(kb, vtb, c, ch, masks[ch % n_sub])
                      for ch in range(2 * n_sub) if masks[ch % n_sub] is not _SKIP]

        def scores(step):
            kb, _, _, ch, mask = step
            s = _dot(kb, qbd_ref[:, ch * cw:(ch + 1) * cw])
            return s if mask is None else jnp.where(mask, s, NEG_BIG)

        def update(step, s):
            _, vtb, c, ch, _ = step
            lanes = slice(ch * cw, (ch + 1) * cw)
            m_prev = m_ref[:, lanes]
            m_next = jnp.maximum(m_prev, jnp.max(s, axis=0, keepdims=True) + c)
            p = jnp.exp2(s - (m_next - c))
            alpha = jnp.exp2(m_prev - m_next)
            m_ref[:, lanes] = m_next
            acc_ref[:, lanes] = alpha * acc_ref[:, lanes] + _dot(vtb, p.astype(jnp.bfloat16))

        ahead = ATTN_SCORES_AHEAD
        slots = s_ref.shape[0]
        for n in range(min(ahead, len(steps))):
            s_ref[n % slots] = scores(steps[n])
        for n, step in enumerate(steps):
            if n + ahead < len(steps):
                s_ref[(n + ahead) % slots] = scores(steps[n + ahead])
            update(step, s_ref[n % slots])

    blocks_per_tile = bq // bk
    unroll = ATTN_BLOCKS_PER_ITER

    def body(jj, carry):
        blocks = []
        for u in range(unroll):
            j = jj * unroll + u
            blocks.append((j, slope * (j * bk - i * bq).astype(jnp.float32), [None] * n_sub))
        run_blocks(blocks)
        return carry

    lax.fori_loop(0, i * (blocks_per_tile // unroll), body, 0)

    kk = lax.broadcasted_iota(jnp.int32, (bk, cw), 0)
    qq = lax.broadcasted_iota(jnp.int32, (bk, cw), 1)
    diag_blocks = []
    for d in range(blocks_per_tile):
        masks = []
        for qs in range(n_sub):
            lo_key, hi_key = d * bk, d * bk + bk - 1
            lo_q, hi_q = qs * cw, qs * cw + cw - 1
            if hi_key <= lo_q:
                masks.append(None)
            elif lo_key > hi_q:
                masks.append(_SKIP)
            else:
                masks.append(kk + lo_key <= qq + lo_q)
        diag_blocks.append((i * blocks_per_tile + d, slope * float(d * bk), masks))
    run_blocks(diag_blocks)

    lam = (jnp.exp(jnp.sum(lq1_ref[...] * lk1_ref[...], axis=1, keepdims=True))
           - jnp.exp(jnp.sum(lq2_ref[...] * lk2_ref[...], axis=1, keepdims=True))
           + lam_init)
    o = acc_ref[:HEAD_W, :] / acc_ref[HEAD_W:HEAD_W + 1, :]
    a = o[:, :bq] - lam * o[:, bq:]
    a = a * lax.rsqrt(jnp.mean(a * a, axis=0, keepdims=True) + NORM_EPS) * subg_ref[...] * (1.0 - lam_init)
    o_ref[...] = a.T.astype(o_ref.dtype)


def _diff_attn(proj, lq1, lk1, lq2, lk2, subg, lam_init):
    b, s, _ = proj.shape
    bq, bk, nh = ATTN_BQ, ATTN_BK, N_DIFF_HEADS
    slopes = jnp.exp2(-8.0 * jnp.arange(1, nh + 1, dtype=jnp.float32) / nh) * LOG2E
    kern = functools.partial(_diff_attn_kernel, bq=bq, bk=bk, lam_init=lam_init)
    small = lambda n: pl.BlockSpec((1, n), lambda bi, h, i: (0, 0))
    return pl.pallas_call(
        kern,
        grid=(b, nh, s // bq),
        in_specs=[
            pl.BlockSpec(memory_space=pltpu.SMEM),
            small(DH_DIFF), small(DH_DIFF), small(DH_DIFF), small(DH_DIFF),
            pl.BlockSpec((HEAD_W, 1), lambda bi, h, i: (0, 0)),
            pl.BlockSpec((None, bq, HEAD_W), lambda bi, h, i: (bi, i, h)),
            pl.BlockSpec((None, s, HEAD_W), lambda bi, h, i: (bi, 0, nh + h)),
            pl.BlockSpec((None, s, HEAD_W), lambda bi, h, i: (bi, 0, 2 * nh + h)),
        ],
        out_specs=pl.BlockSpec((None, bq, HEAD_W), lambda bi, h, i: (bi, i, h)),
        out_shape=jax.ShapeDtypeStruct((b, s, DIFF_WIDTH), jnp.bfloat16),
        scratch_shapes=[
            pltpu.VMEM((2 * HEAD_W, 2 * bq), jnp.bfloat16),
            pltpu.VMEM((bk, HEAD_W), jnp.bfloat16),
            pltpu.VMEM((1, 2 * bq), jnp.float32),
            pltpu.VMEM((HEAD_W + ACC_PAD, 2 * bq), jnp.float32),
            pltpu.VMEM((ATTN_SCORES_AHEAD + 1, bk, ATTN_CW), jnp.float32),
        ],
        compiler_params=pltpu.CompilerParams(
            dimension_semantics=("arbitrary", "arbitrary", "arbitrary"),
            vmem_limit_bytes=VMEM_LIMIT_BYTES),
        name="diff_attn",
    )(slopes, lq1, lk1, lq2, lk2, subg, proj, proj, proj)


def _merge_kernel(x_ref, a_ref, u_ref, uh_ref, qm_ref, g0_ref, g1_ref, g2_ref, km_ref, vm_ref,
                  wa_ref, pw_ref, ps_ref, wp_ref, wm_ref, wo_ref, gpost_ref, o_ref, *, tm):
    i = pl.program_id(1)

    y_attn = _dot(a_ref[...], wa_ref[...])

    u = u_ref[...].astype(jnp.float32)
    halo = jnp.where(i > 0, uh_ref[...].astype(jnp.float32), 0.0)
    t = i * tm + lax.broadcasted_iota(jnp.int32, (tm, POOL_GROUP_WIDTH), 0)
    wsum = jnp.concatenate([halo, u], axis=0)
    ys = []
    for g, w in enumerate(POOL_WINDOWS):
        wsum = wsum + pltpu.roll(wsum, w // 2, axis=0)
        ug = u[:, g * POOL_GROUP_WIDTH:(g + 1) * POOL_GROUP_WIDTH]
        count = jnp.minimum(t + 1, w).astype(jnp.float32)
        pooled = wsum[POOL_HALO:, :POOL_GROUP_WIDTH] / count - ug
        wsum = wsum[:, POOL_GROUP_WIDTH:]
        ys.append(_dot(pooled.astype(jnp.bfloat16), pw_ref[g]))
    y = jnp.concatenate(ys, axis=1) * ps_ref[...]
    y_pool = _dot(y.astype(jnp.bfloat16), wp_ref[...])

    outs = []
    for hh in range(N_MEM_HEADS):
        cols = slice(hh * DH_MEM, (hh + 1) * DH_MEM)
        s = _dot_nt(qm_ref[:, cols], km_ref[:, cols])
        p = jnp.exp2(s - jnp.max(s, axis=1, keepdims=True))
        o = _dot(p.astype(jnp.bfloat16), vm_ref[:, cols])
        outs.append(o / jnp.sum(p, axis=1, keepdims=True))
    y_mem = _dot(jnp.concatenate(outs, axis=1).astype(jnp.bfloat16), wm_ref[...])

    mix = y_attn + y_attn * jnp.tanh(g0_ref[...].astype(jnp.float32))
    mix = mix + y_pool + y_pool * jnp.tanh(g1_ref[...].astype(jnp.float32))
    mix = mix + y_mem + y_mem * jnp.tanh(g2_ref[...].astype(jnp.float32))
    z = _dot((0.5 * mix).astype(jnp.bfloat16), wo_ref[...])
    o_ref[...] = x_ref[...] + _rms_norm(z, gpost_ref[...])


def _merge(x, a, proj, km, vm, wa, pw, ps, wp, wm, wo, gpost):
    b, s, d = x.shape
    tm = MERGE_TM
    m = km.shape[1]
    halo_blocks = tm // POOL_HALO
    pool_col = 3 * DIFF_WIDTH // POOL_WIDTH
    qm_col = pool_col + 1
    gate_col = (3 * DIFF_WIDTH + POOL_WIDTH + MEM_WIDTH) // d
    const2 = lambda shape: pl.BlockSpec(shape, lambda bi, i: (0, 0), pipeline_mode=pl.Buffered(1))
    tile = lambda w, c: pl.BlockSpec((None, tm, w), lambda bi, i, c=c: (bi, i, c))
    kern = functools.partial(_merge_kernel, tm=tm)
    return pl.pallas_call(
        kern,
        grid=(b, s // tm),
        in_specs=[
            tile(d, 0),
            tile(DIFF_WIDTH, 0),
            tile(POOL_WIDTH, pool_col),
            pl.BlockSpec((None, POOL_HALO, POOL_WIDTH),
                         lambda bi, i: (bi, jnp.maximum(i * halo_blocks - 1, 0), pool_col)),
            tile(MEM_WIDTH, qm_col),
            tile(d, gate_col), tile(d, gate_col + 1), tile(d, gate_col + 2),
            pl.BlockSpec((None, m, MEM_WIDTH), lambda bi, i: (bi, 0, 0)),
            pl.BlockSpec((None, m, MEM_WIDTH), lambda bi, i: (bi, 0, 0)),
            const2((DIFF_WIDTH, d)),
            pl.BlockSpec(pw.shape, lambda bi, i: (0, 0, 0)),
            const2((1, POOL_WIDTH)),
            const2((POOL_WIDTH, d)),
            const2((MEM_WIDTH, d)),
            const2((d, d)),
            const2((1, d)),
        ],
        out_specs=tile(d, 0),
        out_shape=jax.ShapeDtypeStruct((b, s, d), jnp.float32),
        compiler_params=pltpu.CompilerParams(
            dimension_semantics=("arbitrary", "arbitrary"), vmem_limit_bytes=VMEM_LIMIT_BYTES),
        name="merge",
    )(x, a, proj, proj, proj, proj, proj, proj, km, vm, wa, pw, ps, wp, wm, wo, gpost)


def _conv_ffn_kernel(x_ref, gpre_ref, wup_ref, cw_ref, cb_ref, wdn_ref, gpost_ref, o_ref,
                     carry_ref, act_ref, *, tm, chunk):
    i = pl.program_id(1)

    @pl.when(i == 0)
    def _():
        carry_ref[...] = jnp.zeros(carry_ref.shape, jnp.float32)

    x = x_ref[...]
    h = _rms_norm(x, gpre_ref[...]).astype(jnp.bfloat16)

    def conv(c0):
        up = _dot(h, wup_ref[:, c0:c0 + chunk])
        ext = jnp.concatenate([carry_ref[:, c0:c0 + chunk], up], axis=0)
        carry_ref[:, c0:c0 + chunk] = up[tm - CONV_HALO:]
        w = cw_ref[:, c0:c0 + chunk]
        y = cb_ref[:, c0:c0 + chunk] + w[CONV_WIDTH - 1:CONV_WIDTH] * up
        for j in range(CONV_WIDTH - 1):
            off = CONV_HALO - (CONV_WIDTH - 1) + j
            y = y + w[j:j + 1] * ext[off:off + tm]
        return y

    for c0 in range(0, D_FF, chunk):
        gate = conv(c0)
        val = conv(D_FF + c0)
        act_ref[:, c0:c0 + chunk] = (jax.nn.gelu(gate, approximate=True) * val).astype(jnp.bfloat16)

    ff = _dot(act_ref[...], wdn_ref[...])
    o_ref[...] = x + _rms_norm(ff, gpost_ref[...])


def _conv_ffn(x, gpre, wup, cw, cb, wdn, gpost):
    b, s, d = x.shape
    tm = FFN_TM
    const2 = lambda shape: pl.BlockSpec(shape, lambda bi, i: (0, 0), pipeline_mode=pl.Buffered(1))
    tile = pl.BlockSpec((None, tm, d), lambda bi, i: (bi, i, 0))
    kern = functools.partial(_conv_ffn_kernel, tm=tm, chunk=FFN_CHUNK)
    return pl.pallas_call(
        kern,
        grid=(b, s // tm),
        in_specs=[
            tile,
            const2((1, d)),
            const2((d, 2 * D_FF)),
            const2((CONV_WIDTH, 2 * D_FF)),
            const2((1, 2 * D_FF)),
            const2((D_FF, d)),
            const2((1, d)),
        ],
        out_specs=tile,
        out_shape=jax.ShapeDtypeStruct((b, s, d), jnp.float32),
        scratch_shapes=[
            pltpu.VMEM((CONV_HALO, 2 * D_FF), jnp.float32),
            pltpu.VMEM((tm, D_FF), jnp.bfloat16),
        ],
        compiler_params=pltpu.CompilerParams(
            dimension_semantics=("arbitrary", "arbitrary"), vmem_limit_bytes=VMEM_LIMIT_BYTES),
        name="conv_ffn",
    )(x, gpre, wup, cw, cb, wdn, gpost)


def kernel(x, mem, norm_mix_pre, w_in, lambda_q1, lambda_k1, lambda_q2, lambda_k2, subln_g, w_attn_branch, pool_w, pool_scale, w_pool_branch, norm_mem, w_mem_kv, w_mem_branch, w_out, norm_mix_post, norm_ffn_pre, w_up, conv_w, conv_b, w_down, norm_ffn_post):
    b, s, d = x.shape
    depth = w_in.shape[0]
    bf = lambda w: w.astype(jnp.bfloat16)
    for l in range(depth):
        lam_init = 0.8 - 0.6 * math.exp(-0.3 * l)
        proj = _in_proj(x.reshape(b * s, d), norm_mix_pre[l][None], w_in[l]).reshape(b, s, IN_COLS)
        km, vm = _mem_kv(mem, norm_mem[l][None], bf(w_mem_kv[l]))
        a = _diff_attn(proj, lambda_q1[l][None], lambda_k1[l][None], lambda_q2[l][None],
                       lambda_k2[l][None], subln_g[l][:, None], lam_init)
        x = _merge(x, a, proj, km, vm, bf(w_attn_branch[l]), bf(pool_w[l]), pool_scale[l][None],
                   bf(w_pool_branch[l]), bf(w_mem_branch[l]), bf(w_out[l]), norm_mix_post[l][None])
        x = _conv_ffn(x, norm_ffn_pre[l][None], bf(w_up[l]), conv_w[l], conv_b[l][None],
                      bf(w_down[l]), norm_ffn_post[l][None])
    return x
```
